```python
import math
import jax
import jax.numpy as jnp
from jax import lax
import numpy as np

D_MODEL = 1024
BATCH = 8
SEQ = 4096
DEPTH = 2

GRID_W = 64
CTX_LEN = 256
EPS = 1e-6
ATT_HEADS = 8
ATT_KV_HEADS = 2
ATT_GROUP = ATT_HEADS // ATT_KV_HEADS
HEAD_DIM = 64
ROPE_PAIRS = HEAD_DIM // 4
ROPE_THETA = 10000.0
Q_BLOCK = 128
A_Q = ATT_HEADS * HEAD_DIM
A_KV = ATT_KV_HEADS * HEAD_DIM
GM_GROUPS = 4
GM_CHUNK = 128
GM_GROUP_CH = 128
GM_CH = GM_GROUPS * GM_GROUP_CH
CV_CH = 512
CV_WIDTH = 31
DN_HEADS = 4
DN_DK = 128
DN_DV = 128
DN_CONV = 5
DN_CHUNK = 64
N_DIR = 2
DN_QK = DN_HEADS * DN_DK
DN_V = DN_HEADS * DN_DV
DN_QKV = 2 * DN_QK + DN_V
DN_AB = 2 * N_DIR * DN_HEADS
AB_IN = A_Q + 2 * A_KV + 2 * GM_CH
AB_OUT = A_Q + GM_CH
CD_IN = 2 * CV_CH + DN_QKV + DN_AB + DN_V
CD_OUT = CV_CH + DN_V
N_AB = (DEPTH + 1) // 2
N_CD = DEPTH // 2
N_EXPERTS = 16
N_GROUPS = 4
EXPERTS_PER_GROUP = N_EXPERTS // N_GROUPS
TOP_K = 2
EXPERT_FF = 512

kernel_name = 'hybrid_dit_gqa_gmlp_conformer_deltanet_moe'


def rms_norm(x, g):
    xf = x.astype(jnp.float32)
    y = xf * lax.rsqrt(jnp.mean(xf * xf, axis=-1, keepdims=True) + EPS)
    return (y * g.astype(jnp.float32)).astype(x.dtype)


def layer_norm(x, g, b):
    xf = x.astype(jnp.float32)
    mu = jnp.mean(xf, axis=-1, keepdims=True)
    var = jnp.mean(jnp.square(xf - mu), axis=-1, keepdims=True)
    y = (xf - mu) * lax.rsqrt(var + EPS)
    return (y * g.astype(jnp.float32) + b.astype(jnp.float32)).astype(x.dtype)


def l2_normalize(x):
    return x * lax.rsqrt(jnp.sum(x * x, axis=-1, keepdims=True) + EPS)


def modulate(h, shift, scale):
    return h * (1.0 + scale) + shift


def depthwise_conv(x, w):
    k = w.shape[0]
    return lax.conv_general_dilated(
        x, w[:, None, :].astype(x.dtype), (1,), [(k // 2, k // 2)],
        dimension_numbers=('NWC', 'WIO', 'NWC'), feature_group_count=w.shape[1])


def apply_axial_rope(x, cos, sin):
    b, s, h, d = x.shape
    xr = x.astype(jnp.float32).reshape(b, s, h, 2, 2, d // 4)
    x1, x2 = xr[..., 0, :], xr[..., 1, :]
    out = jnp.stack([x1 * cos - x2 * sin, x2 * cos + x1 * sin], axis=-2)
    return out.reshape(b, s, h, d).astype(x.dtype)


def softmax_attend(q, k, v):
    s = jnp.einsum('bqhgd,bshd->bhgqs', q, k, preferred_element_type=jnp.float32) * (HEAD_DIM ** -0.5)
    p = jax.nn.softmax(s, axis=-1).astype(v.dtype)
    return jnp.einsum('bhgqs,bshd->bqhgd', p, v)


def blocked_latent_attention(q, k_all, v_all):
    b, s, h, d = q.shape
    qb = q.reshape(b, s // Q_BLOCK, Q_BLOCK, ATT_KV_HEADS, ATT_GROUP, d).transpose(1, 0, 2, 3, 4, 5)
    o = lax.map(lambda qi: softmax_attend(qi, k_all, v_all), qb)
    return o.transpose(1, 0, 2, 3, 4, 5).reshape(b, s, h * d)


def spatial_gating(u, v, ln_g, ln_b, w_s, b_s):
    b, t, _ = v.shape
    v = layer_norm(v, ln_g, ln_b).reshape(b, t // GM_CHUNK, GM_CHUNK, GM_GROUPS, GM_GROUP_CH)
    mixed = jnp.einsum('gpq,bnqgc->bnpgc', w_s, v) + b_s.T[:, :, None]
    return u * mixed.reshape(b, t, GM_CH)


def attn_gmlp_mixer(h_lat, h_ctx, w_in, q_norm, k_norm, ln_g, ln_b, w_s, b_s, w_out, cos, sin, need_ctx):
    cuts = [A_Q, A_Q + A_KV, A_Q + 2 * A_KV, A_Q + 2 * A_KV + GM_CH]

    def project(h, with_gate):
        b, t, _ = h.shape
        q, k, v, u, vg = jnp.split(h @ w_in, cuts, axis=-1)
        q = rms_norm(q.reshape(b, t, ATT_HEADS, HEAD_DIM), q_norm)
        k = rms_norm(k.reshape(b, t, ATT_KV_HEADS, HEAD_DIM), k_norm)
        v = v.reshape(b, t, ATT_KV_HEADS, HEAD_DIM)
        sg = spatial_gating(jax.nn.gelu(u), jax.nn.gelu(vg), ln_g, ln_b, w_s, b_s) if with_gate else None
        return q, k, v, sg

    ql, kl, vl, sl = project(h_lat, True)
    qc, kc, vc, sc = project(h_ctx, need_ctx)
    ql = apply_axial_rope(ql, cos, sin)
    kl = apply_axial_rope(kl, cos, sin)
    k_all = jnp.concatenate([kc, kl], axis=1)
    v_all = jnp.concatenate([vc, vl], axis=1)
    o_lat = blocked_latent_attention(ql, k_all, v_all)
    out_lat = jnp.concatenate([o_lat, sl], axis=-1) @ w_out
    if not need_ctx:
        return out_lat, None
    b, cl = qc.shape[0], qc.shape[1]
    o_ctx = softmax_attend(qc.reshape(b, cl, ATT_KV_HEADS, ATT_GROUP, HEAD_DIM), kc, vc).reshape(b, cl, A_Q)
    out_ctx = jnp.concatenate([o_ctx, sc], axis=-1) @ w_out
    return out_lat, out_ctx


def conformer_branch(a, gate, dw_w, dw_b, ln_g, ln_b):
    y = a * jax.nn.sigmoid(gate)
    y = depthwise_conv(y, dw_w) + dw_b
    return jax.nn.silu(layer_norm(y, ln_g, ln_b))


def delta_features(qkv, ab, conv_w, a_log, dt_bias):
    b, t, _ = qkv.shape
    y = jax.nn.silu(depthwise_conv(qkv, conv_w)).astype(jnp.float32)
    q, k, v = jnp.split(y, [DN_QK, 2 * DN_QK], axis=-1)
    q = l2_normalize(q.reshape(b, t, DN_HEADS, DN_DK)) * (DN_DK ** -0.5)
    k = l2_normalize(k.reshape(b, t, DN_HEADS, DN_DK))
    v = v.reshape(b, t, DN_HEADS, DN_DV)
    abr = ab.astype(jnp.float32).reshape(b, t, 2, N_DIR, DN_HEADS)
    g = -jnp.exp(a_log.astype(jnp.float32)) * jax.nn.softplus(abr[:, :, 0] + dt_bias.astype(jnp.float32))
    beta = jax.nn.sigmoid(abr[:, :, 1])
    return q, k, v, g, beta


def gated_delta_rule(q, k, v, g, beta, state0):
    b, t, h, dk = q.shape
    dv = v.shape[-1]
    L = DN_CHUNK
    n = t // L
    qc = q.reshape(b, n, L, h, dk).transpose(1, 0, 3, 2, 4)
    kc = k.reshape(b, n, L, h, dk).transpose(1, 0, 3, 2, 4)
    vc = v.reshape(b, n, L, h, dv).transpose(1, 0, 3, 2, 4)
    gc = jnp.cumsum(g.reshape(b, n, L, h).transpose(1, 0, 3, 2), axis=-1)
    bc = beta.reshape(b, n, L, h).transpose(1, 0, 3, 2)
    incl = jnp.tril(jnp.ones((L, L), dtype=bool))
    strict = jnp.tril(jnp.ones((L, L), dtype=bool), -1)
    decay = jnp.exp(jnp.where(incl, gc[..., :, None] - gc[..., None, :], -jnp.inf))
    kk = jnp.einsum('nbhid,nbhjd->nbhij', kc, kc)
    lower = jnp.where(strict, bc[..., :, None] * kk * decay, 0.0)
    rhs = jnp.concatenate([vc * bc[..., None], kc * (bc * jnp.exp(gc))[..., None]], axis=-1)
    sol = lax.linalg.triangular_solve(lower, rhs, left_side=True, lower=True, unit_diagonal=True)
    u_c, w_c = sol[..., :dv], sol[..., dv:]
    qk = jnp.einsum('nbhid,nbhjd->nbhij', qc, kc) * decay

    def step(s, inp):
        q_i, k_i, u_i, w_i, g_i, a_i = inp
        v_new = u_i - jnp.einsum('bhlk,bhkv->bhlv', w_i, s)
        o_i = (jnp.einsum('bhlk,bhkv->bhlv', q_i * jnp.exp(g_i)[..., None], s)
               + jnp.einsum('bhij,bhjv->bhiv', a_i, v_new))
        g_last = g_i[..., -1]
        k_dec = k_i * jnp.exp(g_last[..., None] - g_i)[..., None]
        s = s * jnp.exp(g_last)[..., None, None] + jnp.einsum('bhlk,bhlv->bhkv', k_dec, v_new)
        return s, o_i

    s_final, o = lax.scan(step, state0, (qc, kc, u_c, w_c, gc, qk))
    return s_final, o.transpose(1, 0, 3, 2, 4).reshape(b, t, h, dv)


def maybe_flip(t, rev):
    return jnp.flip(t, axis=1) if rev else t


def delta_output(o, gate, o_norm):
    b, t = gate.shape[0], gate.shape[1]
    y = rms_norm(o, o_norm).astype(gate.dtype) * jax.nn.silu(gate.reshape(b, t, DN_HEADS, DN_DV))
    return y.reshape(b, t, DN_V)


def conv_delta_mixer(h_lat, h_ctx, w_in, dw_w, dw_b, ln_g, ln_b, sconv_w, a_log, dt_bias, o_norm, w_out, need_ctx):
    cuts = [CV_CH, 2 * CV_CH, 2 * CV_CH + DN_QKV, 2 * CV_CH + DN_QKV + DN_AB]
    pl = jnp.split(h_lat @ w_in, cuts, axis=-1)
    pc = jnp.split(h_ctx @ w_in, cuts, axis=-1)
    ql, kl, vl, gl, bl = delta_features(pl[2], pl[3], sconv_w, a_log, dt_bias)
    qc, kc, vc, gcx, bcx = delta_features(pc[2], pc[3], sconv_w, a_log, dt_bias)
    b = h_lat.shape[0]
    zero_state = jnp.zeros((b, DN_HEADS, DN_DK, DN_DV), jnp.float32)
    o_lat = jnp.zeros(ql.shape[:3] + (DN_DV,), jnp.float32)
    o_ctx = jnp.zeros(qc.shape[:3] + (DN_DV,), jnp.float32)
    for d in range(N_DIR):
        rev = d == 1
        s_ctx, oc_d = gated_delta_rule(maybe_flip(qc, rev), maybe_flip(kc, rev), maybe_flip(vc, rev),
                                       maybe_flip(gcx[:, :, d], rev), maybe_flip(bcx[:, :, d], rev), zero_state)
        _, ol_d = gated_delta_rule(maybe_flip(ql, rev), maybe_flip(kl, rev), maybe_flip(vl, rev),
                                   maybe_flip(gl[:, :, d], rev), maybe_flip(bl[:, :, d], rev), s_ctx)
        o_lat = o_lat + maybe_flip(ol_d, rev)
        if need_ctx:
            o_ctx = o_ctx + maybe_flip(oc_d, rev)
    conv_lat = conformer_branch(pl[0], pl[1], dw_w, dw_b, ln_g, ln_b)
    out_lat = jnp.concatenate([conv_lat, delta_output(o_lat, pl[4], o_norm)], axis=-1) @ w_out
    if not need_ctx:
        return out_lat, None
    conv_ctx = conformer_branch(pc[0], pc[1], dw_w, dw_b, ln_g, ln_b)
    out_ctx = jnp.concatenate([conv_ctx, delta_output(o_ctx, pc[4], o_norm)], axis=-1) @ w_out
    return out_lat, out_ctx


def moe_ffn(h, router_w, router_b, w_gate, w_up, w_down):
    t = h.shape[0]
    scores = jax.nn.sigmoid(jnp.matmul(h, router_w, preferred_element_type=jnp.float32))
    sel = scores + router_b.astype(jnp.float32)
    group_score = lax.top_k(sel.reshape(t, N_GROUPS, EXPERTS_PER_GROUP), TOP_K)[0].sum(-1)
    best = jnp.argmax(group_score, axis=-1)
    in_group = (jnp.arange(N_EXPERTS) // EXPERTS_PER_GROUP)[None, :] == best[:, None]
    _, idx = lax.top_k(jnp.where(in_group, sel, -jnp.inf), TOP_K)
    w = jnp.take_along_axis(scores, idx, axis=-1)
    w = w / jnp.sum(w, axis=-1, keepdims=True)
    combine = jnp.einsum('tk,tke->te', w, jax.nn.one_hot(idx, N_EXPERTS, dtype=jnp.float32)).astype(h.dtype)
    out = jnp.zeros_like(h)
    for e in range(N_EXPERTS):
        a = jax.nn.silu(h @ w_gate[e]) * (h @ w_up[e])
        out = out + combine[:, e:e + 1] * (a @ w_down[e])
    return out


def setup_inputs(seed: int = 0) -> dict:
    key = jax.random.key(seed)
    ks = iter(jax.random.split(key, 40))
    f32 = jnp.float32

    def nrm(shape, scale):
        return jax.random.normal(next(ks), shape, f32) * scale

    def gain(shape):
        return 1.0 + nrm(shape, 0.02)

    dt = jnp.exp(jax.random.uniform(next(ks), (N_CD, N_DIR, DN_HEADS), f32,
                                    minval=math.log(1e-3), maxval=math.log(1e-1)))
    return {
        'x': nrm((BATCH, SEQ, D_MODEL), 1.0),
        'c': nrm((BATCH, D_MODEL), 1.0),
        'ctx': nrm((BATCH, CTX_LEN, D_MODEL), 1.0),
        'c_ctx': nrm((D_MODEL,), 1.0),
        'mod_w': nrm((DEPTH, D_MODEL, 6 * D_MODEL), 0.5 * D_MODEL ** -0.5),
        'mod_b': nrm((DEPTH, 6 * D_MODEL), 0.02),
        'norm1_g': gain((DEPTH, D_MODEL)),
        'norm2_g': gain((DEPTH, D_MODEL)),
        'ab_w_in': nrm((N_AB, D_MODEL, AB_IN), D_MODEL ** -0.5),
        'ab_q_norm': gain((N_AB, HEAD_DIM)),
        'ab_k_norm': gain((N_AB, HEAD_DIM)),
        'gm_ln_g': gain((N_AB, GM_CH)),
        'gm_ln_b': nrm((N_AB, GM_CH), 0.02),
        'gm_w_s': nrm((N_AB, GM_GROUPS, GM_CHUNK, GM_CHUNK), GM_CHUNK ** -0.5),
        'gm_b_s': gain((N_AB, GM_GROUPS, GM_CHUNK)),
        'ab_w_out': nrm((N_AB, AB_OUT, D_MODEL), AB_OUT ** -0.5),
        'cd_w_in': nrm((N_CD, D_MODEL, CD_IN), D_MODEL ** -0.5),
        'cv_dw_w': nrm((N_CD, CV_WIDTH, CV_CH), CV_WIDTH ** -0.5),
        'cv_dw_b': nrm((N_CD, CV_CH), 0.02),
        'cv_ln_g': gain((N_CD, CV_CH)),
        'cv_ln_b': nrm((N_CD, CV_CH), 0.02),
        'dn_conv_w': nrm((N_CD, DN_CONV, DN_QKV), DN_CONV ** -0.5),
        'dn_a_log': jnp.log(jax.random.uniform(next(ks), (N_CD, N_DIR, DN_HEADS), f32, minval=1.0, maxval=16.0)),
        'dn_dt_bias': dt + jnp.log(-jnp.expm1(-dt)),
        'dn_o_norm': gain((N_CD, DN_DV)),
        'cd_w_out': nrm((N_CD, CD_OUT, D_MODEL), CD_OUT ** -0.5),
        'router_w': nrm((D_MODEL, N_EXPERTS), D_MODEL ** -0.5),
        'router_b': nrm((N_EXPERTS,), 0.01),
        'moe_w_gate': nrm((DEPTH, N_EXPERTS, D_MODEL, EXPERT_FF), D_MODEL ** -0.5),
        'moe_w_up': nrm((DEPTH, N_EXPERTS, D_MODEL, EXPERT_FF), D_MODEL ** -0.5),
        'moe_w_down': nrm((DEPTH, N_EXPERTS, EXPERT_FF, D_MODEL), EXPERT_FF ** -0.5),
        'final_norm_g': gain((D_MODEL,)),
    }


def reference(x, c, ctx, c_ctx, mod_w, mod_b, norm1_g, norm2_g,
              ab_w_in, ab_q_norm, ab_k_norm, gm_ln_g, gm_ln_b, gm_w_s, gm_b_s, ab_w_out,
              cd_w_in, cv_dw_w, cv_dw_b, cv_ln_g, cv_ln_b, dn_conv_w, dn_a_log, dn_dt_bias, dn_o_norm, cd_w_out,
              router_w, router_b, moe_w_gate, moe_w_up, moe_w_down, final_norm_g):
    b, s, d = x.shape
    rows = s // GRID_W
    row = jnp.repeat(jnp.arange(rows, dtype=jnp.int32), GRID_W)
    col = jnp.tile(jnp.arange(GRID_W, dtype=jnp.int32), rows)
    freqs = ROPE_THETA ** (-jnp.arange(ROPE_PAIRS, dtype=jnp.float32) / ROPE_PAIRS)
    ang = jnp.stack([row, col], axis=-1).astype(jnp.float32)[..., None] * freqs
    cos = jnp.cos(ang)[:, None]
    sin = jnp.sin(ang)[:, None]

    x_lat, x_ctx = x, ctx
    silu_c = jax.nn.silu(c)
    silu_cc = jax.nn.silu(c_ctx)
    for layer in range(DEPTH):
        last = layer == DEPTH - 1
        i = layer // 2
        mod = silu_c @ mod_w[layer] + mod_b[layer]
        mod_c = silu_cc @ mod_w[layer] + mod_b[layer]
        sh1, sc1, ga1, sh2, sc2, ga2 = jnp.split(mod[:, None, :], 6, axis=-1)
        csh1, csc1, cga1, csh2, csc2, cga2 = jnp.split(mod_c, 6, axis=-1)
        h_l = modulate(rms_norm(x_lat, norm1_g[layer]), sh1, sc1)
        h_c = modulate(rms_norm(x_ctx, norm1_g[layer]), csh1, csc1)
        if layer % 2 == 0:
            m_l, m_c = attn_gmlp_mixer(h_l, h_c, ab_w_in[i], ab_q_norm[i], ab_k_norm[i], gm_ln_g[i], gm_ln_b[i],
                                       gm_w_s[i], gm_b_s[i], ab_w_out[i], cos, sin, not last)
        else:
            m_l, m_c = conv_delta_mixer(h_l, h_c, cd_w_in[i], cv_dw_w[i], cv_dw_b[i], cv_ln_g[i], cv_ln_b[i],
                                        dn_conv_w[i], dn_a_log[i], dn_dt_bias[i], dn_o_norm[i], cd_w_out[i], not last)
        x_lat = x_lat + ga1 * m_l
        h_l = modulate(rms_norm(x_lat, norm2_g[layer]), sh2, sc2).reshape(-1, d)
        if last:
            f_l = moe_ffn(h_l, router_w, router_b, moe_w_gate[layer], moe_w_up[layer], moe_w_down[layer])
        else:
            x_ctx = x_ctx + cga1 * m_c
            h_c = modulate(rms_norm(x_ctx, norm2_g[layer]), csh2, csc2).reshape(-1, d)
            f = moe_ffn(jnp.concatenate([h_l, h_c], axis=0), router_w, router_b,
                        moe_w_gate[layer], moe_w_up[layer], moe_w_down[layer])
            f_l = f[:b * s]
            x_ctx = x_ctx + cga2 * f[b * s:].reshape(x_ctx.shape)
        x_lat = x_lat + ga2 * f_l.reshape(b, s, d)
    return rms_norm(x_lat, final_norm_g)
```

```python
import functools
import math

import jax
import jax.numpy as jnp
from jax import lax
from jax.experimental import pallas as pl
from jax.experimental.pallas import tpu as pltpu

f32 = jnp.float32
bf16 = jnp.bfloat16
i32 = jnp.int32

GRID_W = 64
EPS = 1e-6
ATT_HEADS = 8
ATT_KV_HEADS = 2
HEAD_DIM = 64
ROPE_PAIRS = HEAD_DIM // 4
ROPE_THETA = 10000.0
A_Q = ATT_HEADS * HEAD_DIM
A_KV = ATT_KV_HEADS * HEAD_DIM
GM_GROUPS = 4
GM_CHUNK = 128
GM_CH = 512
CV_CH = 512
CV_WIDTH = 31
DN_HEADS = 4
DN_DK = 128
DN_CONV = 5
DN_CHUNK = 64
N_DIR = 2
DN_QK = DN_HEADS * DN_DK
DN_QKV = 3 * DN_QK
N_EXPERTS = 16
N_GROUPS = 4
EXPERT_FF = 512

LANES = 128
TM = 256
V_ROWS = HEAD_DIM + 16
VMEM_LIMIT = 48 * 1024 * 1024


def _cparams(sem):
    return pltpu.CompilerParams(dimension_semantics=sem, vmem_limit_bytes=VMEM_LIMIT)


def _sigmoid(x):
    return 1.0 / (1.0 + jnp.exp(-x))


def _silu(x):
    return x * _sigmoid(x)


def _gelu(x):
    return x * (0.5 * (1.0 + jnp.tanh(math.sqrt(2.0 / math.pi) * (x + 0.044715 * (x * x * x)))))


def _rms_rows(x):
    return x * lax.rsqrt(jnp.mean(x * x, axis=-1, keepdims=True) + EPS)


def _layer_norm_rows(x, g, b):
    mu = jnp.mean(x, axis=-1, keepdims=True)
    xc = x - mu
    var = jnp.mean(xc * xc, axis=-1, keepdims=True)
    return xc * lax.rsqrt(var + EPS) * g + b


def _dot(a, b):
    return jnp.dot(a.astype(bf16), b.astype(bf16), preferred_element_type=f32)


def _dot_nt(a, b):
    return lax.dot_general(a.astype(bf16), b.astype(bf16), (((1,), (1,)), ((), ())), preferred_element_type=f32)


def _dot_tn(a, b):
    return lax.dot_general(a.astype(bf16), b.astype(bf16), (((0,), (0,)), ((), ())), preferred_element_type=f32)


def _mod_kernel(c_ref, w_ref, b_ref, o_ref):
    c = c_ref[...]
    o_ref[...] = jnp.dot(_silu(c), w_ref[...], precision=lax.Precision.HIGHEST,
                         preferred_element_type=f32) + b_ref[...]


def _modulation(cc, mod_w, mod_b):
    depth, d, n = mod_w.shape
    tn = 1024
    rows = cc.shape[0]
    return pl.pallas_call(
        _mod_kernel,
        grid=(depth, n // tn),
        in_specs=[pl.BlockSpec((rows, d), lambda l, j: (0, 0)),
                  pl.BlockSpec((None, d, tn), lambda l, j: (l, 0, j)),
                  pl.BlockSpec((None, 1, tn), lambda l, j: (l, 0, j))],
        out_specs=pl.BlockSpec((None, rows, tn), lambda l, j: (l, 0, j)),
        out_shape=jax.ShapeDtypeStruct((depth, rows, n), f32),
        compiler_params=_cparams(("arbitrary", "arbitrary")),
        name="modulation",
    )(cc, mod_w, mod_b.reshape(depth, 1, n))


def _mod_index(b, i):
    return (b, jnp.minimum(i, 1), 0, 0)


def _head_norm_rope(x, seg_mean, gain, cos, sin, scale):
    xx = x * x
    outs = []
    lane = lax.broadcasted_iota(i32, (x.shape[0], LANES), 1)
    first = (lane % 32) < 16
    for c0 in range(0, x.shape[1], 2 * LANES):
        w = min(2 * LANES, x.shape[1] - c0)
        ms = _dot(xx[:, c0:c0 + w], seg_mean[0:w, 0:w])
        xn = x[:, c0:c0 + w] * lax.rsqrt(ms + EPS)
        for c1 in range(0, w, LANES):
            y = xn[:, c1:c1 + LANES] * gain
            partner = jnp.where(first, pltpu.roll(y, LANES - 16, 1), pltpu.roll(y, 16, 1))
            outs.append((y * cos + partner * sin) * scale)
    return outs


def _ab_in_kernel(x_ref, mod_ref, g_ref, w_ref, seg_ref, qg_ref, kg_ref, cos_ref, sin_ref,
                  lng_ref, lnb_ref, ws_ref, bs_ref, q_ref, kd_ref, vt_ref, s_ref):
    mod = mod_ref[...]
    h = _rms_rows(x_ref[...]) * g_ref[...] * (1.0 + mod[1:2]) + mod[0:1]
    acc = _dot(h, w_ref[...])
    cos = cos_ref[...]
    sin = sin_ref[...]
    seg = seg_ref[...]

    q = _head_norm_rope(acc[:, 0:A_Q], seg, qg_ref[...], cos, sin, HEAD_DIM ** -0.5)
    for c, qc in enumerate(q):
        q_ref[:, c * LANES:(c + 1) * LANES] = qc.astype(bf16)

    (k,) = _head_norm_rope(acc[:, A_Q:A_Q + A_KV], seg, kg_ref[...], cos, sin, 1.0)
    lane = lax.broadcasted_iota(i32, k.shape, 1)
    swapped = pltpu.roll(k, HEAD_DIM, 1)
    kd_ref[0] = jnp.where(lane < HEAD_DIM, k, swapped).astype(bf16)
    kd_ref[1] = jnp.where(lane < HEAD_DIM, swapped, k).astype(bf16)

    vt = acc[:, A_Q + A_KV:A_Q + 2 * A_KV].T
    ones = jnp.ones((V_ROWS - HEAD_DIM, vt.shape[1]), bf16)
    for j in range(ATT_KV_HEADS):
        vt_ref[j, 0:HEAD_DIM, :] = vt[j * HEAD_DIM:(j + 1) * HEAD_DIM].astype(bf16)
        vt_ref[j, HEAD_DIM:V_ROWS, :] = ones

    c0 = A_Q + 2 * A_KV
    gu = _gelu(acc[:, c0:c0 + GM_CH])
    ln = _layer_norm_rows(_gelu(acc[:, c0 + GM_CH:c0 + 2 * GM_CH]), lng_ref[...], lnb_ref[...])
    for n in range(acc.shape[0] // GM_CHUNK):
        r = slice(n * GM_CHUNK, (n + 1) * GM_CHUNK)
        for g in range(GM_GROUPS):
            cs = slice(g * LANES, (g + 1) * LANES)
            mixed = _dot(ws_ref[g], ln[r, cs]) + bs_ref[g]
            s_ref[r, cs] = (gu[r, cs] * mixed).astype(bf16)


def _ab_in_proj(x, modl, g, w, seg, qg, kg, cos, sin, lng, lnb, ws, bs):
    b, l, d = x.shape
    nt = l // TM
    n = w.shape[1]
    full = lambda shape: pl.BlockSpec(shape, lambda bi, i: (0,) * len(shape))
    return pl.pallas_call(
        _ab_in_kernel,
        grid=(b, nt),
        in_specs=[pl.BlockSpec((None, TM, d), lambda bi, i: (bi, i, 0)),
                  pl.BlockSpec((None, None, 6, d), _mod_index),
                  full((1, d)), full((d, n)), full((2 * LANES, 2 * LANES)),
                  full((1, LANES)), full((1, LANES)),
                  pl.BlockSpec((TM, LANES), lambda bi, i: (i, 0)),
                  pl.BlockSpec((TM, LANES), lambda bi, i: (i, 0)),
                  full((1, GM_CH)), full((1, GM_CH)),
                  full((GM_GROUPS, GM_CHUNK, GM_CHUNK)), full((GM_GROUPS, GM_CHUNK, LANES))],
        out_specs=[pl.BlockSpec((None, TM, A_Q), lambda bi, i: (bi, i, 0)),
                   pl.BlockSpec((None, ATT_KV_HEADS, TM, LANES), lambda bi, i: (bi, 0, i, 0)),
                   pl.BlockSpec((None, ATT_KV_HEADS, None, V_ROWS, TM), lambda bi, i: (bi, 0, i, 0, 0)),
                   pl.BlockSpec((None, TM, GM_CH), lambda bi, i: (bi, i, 0))],
        out_shape=[jax.ShapeDtypeStruct((b, l, A_Q), bf16),
                   jax.ShapeDtypeStruct((b, ATT_KV_HEADS, l, LANES), bf16),
                   jax.ShapeDtypeStruct((b, ATT_KV_HEADS, nt, V_ROWS, TM), bf16),
                   jax.ShapeDtypeStruct((b, l, GM_CH), bf16)],
        compiler_params=_cparams(("arbitrary", "arbitrary")),
        name="ab_in_proj",
    )(x, modl, g, w, seg, qg, kg, cos, sin, lng, lnb, ws, bs)


def _attn_kernel(q_ref, k_ref, vt_ref, o_ref, acc_ref, m_ref, *, n_pairs):
    i = pl.program_id(2)
    q = q_ref[...]
    lane = lax.broadcasted_iota(i32, (TM, LANES), 1)
    zero = jnp.zeros((TM, LANES), bf16)
    pieces = []
    for p in range(2):
        qp = q[:, p * LANES:(p + 1) * LANES]
        pieces.append(jnp.where(lane < HEAD_DIM, qp, zero))
        pieces.append(jnp.where(lane < HEAD_DIM, zero, qp))
    qs = jnp.concatenate(pieces, axis=0)
    nq = qs.shape[0]
    m_ref[...] = jnp.full((1, nq), -jnp.inf, f32)
    acc_ref[...] = jnp.zeros((V_ROWS, nq), f32)

    def block(kb, vb):
        st = _dot_nt(kb, qs)
        m_old = m_ref[...]
        m_new = jnp.maximum(m_old, jnp.max(st, axis=0, keepdims=True))
        p = jnp.exp(st - m_new).astype(bf16)
        acc_ref[...] = acc_ref[...] * jnp.exp(m_old - m_new) + jnp.dot(vb, p, preferred_element_type=f32)
        m_ref[...] = m_new

    block(k_ref[0:TM, :], vt_ref[0])

    @pl.when(i > 0)
    def _():
        def body(t, carry):
            start = pl.multiple_of(TM + t * (2 * TM), TM)
            vb = jnp.concatenate([vt_ref[1 + 2 * t], vt_ref[2 + 2 * t]], axis=1)
            block(k_ref[pl.ds(start, 2 * TM), :], vb)
            return carry
        lax.fori_loop(0, n_pairs, body, 0)

    acc = acc_ref[...]
    o = acc[0:HEAD_DIM] / acc[HEAD_DIM:HEAD_DIM + 1]
    for p in range(2):
        pair = jnp.concatenate([o[:, (2 * p) * TM:(2 * p + 1) * TM],
                                o[:, (2 * p + 1) * TM:(2 * p + 2) * TM]], axis=0)
        o_ref[:, p * LANES:(p + 1) * LANES] = pair.T.astype(bf16)


def _attention(q, kd, vt):
    b, l, _ = q.shape
    nt = l // TM
    group_w = A_Q // ATT_KV_HEADS
    assert (l - TM) % (2 * TM) == 0
    return pl.pallas_call(
        functools.partial(_attn_kernel, n_pairs=(l - TM) // (2 * TM)),
        grid=(b, ATT_KV_HEADS, nt),
        in_specs=[pl.BlockSpec((None, TM, group_w), lambda bi, j, i: (bi, i, j)),
                  pl.BlockSpec((None, None, l, LANES), lambda bi, j, i: (bi, j, 0, 0)),
                  pl.BlockSpec((None, None, nt, V_ROWS, TM), lambda bi, j, i: (bi, j, 0, 0, 0))],
        out_specs=pl.BlockSpec((None, TM, group_w), lambda bi, j, i: (bi, i, j)),
        out_shape=jax.ShapeDtypeStruct((b, l, A_Q), bf16),
        scratch_shapes=[pltpu.VMEM((V_ROWS, 4 * TM), f32), pltpu.VMEM((1, 4 * TM), f32)],
        compiler_params=_cparams(("arbitrary", "arbitrary", "arbitrary")),
        name="attention",
    )(q, kd, vt)


def _route_rows(scores, sel):
    s = [sel[e:e + 1] for e in range(N_EXPERTS)]
    sc = [scores[e:e + 1] for e in range(N_EXPERTS)]
    per = N_EXPERTS // N_GROUPS
    gs = []
    for g in range(N_GROUPS):
        a, b, c, d = s[per * g:per * g + per]
        hi1, lo1 = jnp.maximum(a, b), jnp.minimum(a, b)
        hi2, lo2 = jnp.maximum(c, d), jnp.minimum(c, d)
        gs.append(jnp.maximum(hi1, hi2) + jnp.maximum(jnp.minimum(hi1, hi2), jnp.maximum(lo1, lo2)))
    best = jnp.zeros(gs[0].shape, i32)
    best_v = gs[0]
    for g in range(1, N_GROUPS):
        better = gs[g] > best_v
        best = jnp.where(better, g, best)
        best_v = jnp.where(better, gs[g], best_v)
    v, w = [], []
    for j in range(per):
        vj, wj = s[j], sc[j]
        for g in range(1, N_GROUPS):
            vj = jnp.where(best == g, s[per * g + j], vj)
            wj = jnp.where(best == g, sc[per * g + j], wj)
        v.append(vj)
        w.append(wj)
    i1 = jnp.zeros(best.shape, i32)
    m1, w1 = v[0], w[0]
    for j in range(1, per):
        better = v[j] > m1
        i1 = jnp.where(better, j, i1)
        m1 = jnp.where(better, v[j], m1)
        w1 = jnp.where(better, w[j], w1)
    i2 = jnp.zeros(best.shape, i32)
    m2 = jnp.full(m1.shape, -jnp.inf, f32)
    w2 = jnp.zeros(m1.shape, f32)
    for j in range(per):
        cand = jnp.where(i1 == j, -jnp.inf, v[j])
        better = cand > m2
        i2 = jnp.where(better, j, i2)
        m2 = jnp.where(better, cand, m2)
        w2 = jnp.where(better, w[j], w2)
    tot = w1 + w2
    return best * per + i1, best * per + i2, w1 / tot, w2 / tot


def _out_kernel(*refs, delta):
    if delta:
        (a1_ref, of_ref, ob_ref, dg_ref, on_ref, x_ref, mod_ref, g_ref, w_ref, rw_ref, rb_ref,
         xo_ref, h_ref, idx_ref, wt_ref) = refs
        o = of_ref[...] + ob_ref[...]
        dg = dg_ref[...]
        parts = []
        for h in range(DN_HEADS):
            cs = slice(h * LANES, (h + 1) * LANES)
            parts.append(_rms_rows(o[:, cs]) * on_ref[...] * _silu(dg[:, cs]))
        a2 = jnp.concatenate(parts, axis=1)
    else:
        (a1_ref, a2_ref, x_ref, mod_ref, g_ref, w_ref, rw_ref, rb_ref,
         xo_ref, h_ref, idx_ref, wt_ref) = refs
        a2 = a2_ref[...]
    half = a1_ref.shape[-1]
    y = _dot(a1_ref[...], w_ref[0:half, :]) + _dot(a2, w_ref[half:2 * half, :])
    mod = mod_ref[...]
    xn = x_ref[...] + mod[2:3] * y
    xo_ref[...] = xn
    h2 = _rms_rows(xn) * g_ref[...] * (1.0 + mod[4:5]) + mod[3:4]
    h_ref[...] = h2
    logits = jnp.dot(h2, rw_ref[...], precision=lax.Precision.HIGHEST, preferred_element_type=f32)
    scores = _sigmoid(logits.T[0:N_EXPERTS])
    e1, e2, w1, w2 = _route_rows(scores, scores + rb_ref[...])
    idx_ref[0:1, :] = e1
    idx_ref[1:2, :] = e2
    wt_ref[0:1, :] = w1
    wt_ref[1:2, :] = w2


def _out_proj(a1, a2s, x, modl, g, w, rw, rb, *, delta, lat_only):
    b, l, d = x.shape
    nt = l // TM
    off = 1 if lat_only else 0
    nto = nt - off
    half = a1.shape[-1]
    full = lambda shape: pl.BlockSpec(shape, lambda bi, i: (0,) * len(shape))
    row_l = lambda w_: pl.BlockSpec((None, TM, w_), lambda bi, i: (bi, i + off, 0))
    if delta:
        o, dg, on = a2s
        a2_specs = [pl.BlockSpec((None, None, TM, half), lambda bi, i: (0, bi, i + off, 0)),
                    pl.BlockSpec((None, None, TM, half), lambda bi, i: (1, bi, i + off, 0)),
                    row_l(half), full((1, LANES))]
        a2_args = [o, o, dg, on]
        a1_spec = pl.BlockSpec((None, TM, half), lambda bi, i: (bi, i, 0))
    else:
        a2_specs = [row_l(half)]
        a2_args = [a2s]
        a1_spec = row_l(half)
    t_out = b * nto * TM
    flat = lambda bi, i: (bi * nto + i, 0)
    return pl.pallas_call(
        functools.partial(_out_kernel, delta=delta),
        grid=(b, nto),
        in_specs=[a1_spec] + a2_specs + [
            row_l(d),
            pl.BlockSpec((None, None, 6, d), lambda bi, i: (bi, jnp.minimum(i + off, 1), 0, 0)),
            full((1, d)), full((2 * half, d)), full((d, LANES)), full((N_EXPERTS, 1))],
        out_specs=[pl.BlockSpec((TM, d), flat), pl.BlockSpec((TM, d), flat),
                   pl.BlockSpec((2, TM), lambda bi, i: (0, bi * nto + i)),
                   pl.BlockSpec((2, TM), lambda bi, i: (0, bi * nto + i))],
        out_shape=[jax.ShapeDtypeStruct((t_out, d), f32), jax.ShapeDtypeStruct((t_out, d), f32),
                   jax.ShapeDtypeStruct((2, t_out), i32), jax.ShapeDtypeStruct((2, t_out), f32)],
        compiler_params=_cparams(("arbitrary", "arbitrary")),
        name="out_proj_delta" if delta else "out_proj",
    )(a1, *a2_args, x, modl, g, w, rw, rb)


def _dispatch_plan(idx):
    t = idx.shape[1]
    e = idx.reshape(-1)
    onehot = (e[:, None] == jnp.arange(N_EXPERTS, dtype=i32)[None, :]).astype(i32)
    csum = jnp.cumsum(onehot, axis=0)
    rank = jnp.sum(csum * onehot, axis=1) - 1
    counts = csum[-1]
    padded = ((counts + TM - 1) // TM) * TM
    ends = jnp.cumsum(padded)
    base = ends - padded
    pos = jnp.sum(base[None, :] * onehot, axis=1) + rank
    p_rows = 2 * t + N_EXPERTS * TM
    n_tiles = p_rows // TM
    tok = jnp.tile(jnp.arange(t, dtype=i32), 2)
    src = jnp.zeros((p_rows,), i32).at[pos].set(tok)
    n_used = (ends[-1] // TM).astype(i32)
    tiles = jnp.arange(n_tiles, dtype=i32)
    tile_e = jnp.sum((tiles[:, None] * TM >= ends[None, :]).astype(i32), axis=1)
    last_e = jnp.sum(((n_used - 1) * TM >= ends).astype(i32))
    tile_e = jnp.where(tiles < n_used, tile_e, last_e)
    return src.reshape(n_tiles, 1, TM), tile_e, n_used.reshape(1), pos.reshape(2, t)


def _row_gather(idx_ref, n, src_hbm, dst, sem, *, start):
    def body(r, carry):
        row = idx_ref[0, 0, r] if start else 0
        cp = pltpu.make_async_copy(src_hbm.at[pl.ds(row, 1)], dst.at[pl.ds(r, 1)], sem)
        if start:
            cp.start()
        else:
            cp.wait()
        return carry
    lax.fori_loop(0, n, body, 0)


def _moe_ffn_kernel(te_ref, nu_ref, src_ref, srcn_ref, h_hbm, wg_ref, wu_ref, wd_ref, y_ref, xbuf, sem):
    i = pl.program_id(0)
    nu = nu_ref[0]
    slot = i % 2

    @pl.when(i == 0)
    def _():
        _row_gather(src_ref, TM, h_hbm, xbuf.at[0], sem.at[0], start=True)

    @pl.when(i + 1 < nu)
    def _():
        _row_gather(srcn_ref, TM, h_hbm, xbuf.at[1 - slot], sem.at[1 - slot], start=True)

    @pl.when(i < nu)
    def _():
        _row_gather(src_ref, TM, h_hbm, xbuf.at[slot], sem.at[slot], start=False)
        x = xbuf[slot]
        a = _silu(_dot(x, wg_ref[...])) * _dot(x, wu_ref[...])
        y_ref[...] = _dot(a, wd_ref[...])

    @pl.when(i >= nu)
    def _():
        y_ref[...] = jnp.zeros(y_ref.shape, f32)


def _moe_ffn(h, src, tile_e, n_used, wg, wu, wd):
    n_tiles = src.shape[0]
    d = h.shape[1]
    ff = wg.shape[2]
    grid_spec = pltpu.PrefetchScalarGridSpec(
        num_scalar_prefetch=2,
        grid=(n_tiles,),
        in_specs=[pl.BlockSpec((1, 1, TM), lambda i, te, nu: (i, 0, 0), memory_space=pltpu.SMEM),
                  pl.BlockSpec((1, 1, TM), lambda i, te, nu: (jnp.minimum(i + 1, n_tiles - 1), 0, 0),
                               memory_space=pltpu.SMEM),
                  pl.BlockSpec(memory_space=pl.ANY),
                  pl.BlockSpec((None, d, ff), lambda i, te, nu: (te[i], 0, 0)),
                  pl.BlockSpec((None, d, ff), lambda i, te, nu: (te[i], 0, 0)),
                  pl.BlockSpec((None, ff, d), lambda i, te, nu: (te[i], 0, 0))],
        out_specs=pl.BlockSpec((TM, d), lambda i, te, nu: (i, 0)),
        scratch_shapes=[pltpu.VMEM((2, TM, d), f32), pltpu.SemaphoreType.DMA((2,))],
    )
    return pl.pallas_call(
        _moe_ffn_kernel,
        grid_spec=grid_spec,
        out_shape=jax.ShapeDtypeStruct((n_tiles * TM, d), f32),
        compiler_params=_cparams(("arbitrary",)),
        name="moe_ffn",
    )(tile_e, n_used, src, src, h, wg, wu, wd)


def _combine_kernel(pos_ref, posn_ref, y_hbm, x_ref, wt_ref, mod_ref, *rest, final):
    if final:
        fg_ref, o_ref, ybuf, sem = rest
    else:
        o_ref, ybuf, sem = rest
    i = pl.program_id(0)
    n = pl.num_programs(0)
    slot = i % 2

    @pl.when(i == 0)
    def _():
        _row_gather(pos_ref, 2 * TM, y_hbm, ybuf.at[0], sem.at[0], start=True)

    @pl.when(i + 1 < n)
    def _():
        _row_gather(posn_ref, 2 * TM, y_hbm, ybuf.at[1 - slot], sem.at[1 - slot], start=True)

    _row_gather(pos_ref, 2 * TM, y_hbm, ybuf.at[slot], sem.at[slot], start=False)
    w = wt_ref[...]
    f = w[:, 0:1] * ybuf[slot, 0:TM] + w[:, 1:2] * ybuf[slot, TM:2 * TM]
    out = x_ref[...] + mod_ref[5:6] * f
    if final:
        out = _rms_rows(out) * fg_ref[...]
    o_ref[...] = out


def _moe_combine(y, pos, x, wts, modl, tiles_per_batch, lat_only, final_g=None):
    t, d = x.shape
    n = t // TM
    pos_t = pos.reshape(2, n, TM).transpose(1, 0, 2).reshape(n, 1, 2 * TM)
    off = 1 if lat_only else 0
    final = final_g is not None
    in_specs = [pl.BlockSpec((1, 1, 2 * TM), lambda i: (i, 0, 0), memory_space=pltpu.SMEM),
                pl.BlockSpec((1, 1, 2 * TM), lambda i: (jnp.minimum(i + 1, n - 1), 0, 0), memory_space=pltpu.SMEM),
                pl.BlockSpec(memory_space=pl.ANY),
                pl.BlockSpec((TM, d), lambda i: (i, 0)),
                pl.BlockSpec((TM, 2), lambda i: (i, 0)),
                pl.BlockSpec((None, None, 6, d),
                             lambda i: (i // tiles_per_batch, jnp.minimum(i % tiles_per_batch + off, 1), 0, 0))]
    args = [pos_t, pos_t, y, x, wts.T, modl]
    if final:
        in_specs.append(pl.BlockSpec((1, d), lambda i: (0, 0)))
        args.append(final_g)
    return pl.pallas_call(
        functools.partial(_combine_kernel, final=final),
        grid=(n,),
        in_specs=in_specs,
        out_specs=pl.BlockSpec((TM, d), lambda i: (i, 0)),
        out_shape=jax.ShapeDtypeStruct((t, d), f32),
        scratch_shapes=[pltpu.VMEM((2, 2 * TM, d), f32), pltpu.SemaphoreType.DMA((2,))],
        compiler_params=_cparams(("arbitrary",)),
        name="moe_combine_final" if final else "moe_combine",
    )(*args)


def _moe(h, x, idx, wts, modl, wg, wu, wd, tiles_per_batch, lat_only, final_g=None):
    src, tile_e, n_used, pos = _dispatch_plan(idx)
    y = _moe_ffn(h, src, tile_e, n_used, wg, wu, wd)
    return _moe_combine(y, pos, x, wts, modl, tiles_per_batch, lat_only, final_g)


def _cd_in_kernel(x_ref, mod_ref, g_ref, w_ref, ag_ref, qkv_ref, dg_ref, ab_ref):
    mod = mod_ref[...]
    h = _rms_rows(x_ref[...]) * g_ref[...] * (1.0 + mod[1:2]) + mod[0:1]
    acc = _dot(h, w_ref[...])
    c1 = 2 * CV_CH
    c2 = c1 + DN_QKV
    c3 = c2 + DN_QK
    ag_ref[...] = acc[:, 0:c1]
    qkv_ref[...] = acc[:, c1:c2]
    dg_ref[...] = acc[:, c2:c3]
    ab_ref[...] = acc[:, c3:c3 + LANES]


def _cd_in_proj(x, modl, g, w):
    b, l, d = x.shape
    nt = l // TM
    n = w.shape[1]
    full = lambda shape: pl.BlockSpec(shape, lambda bi, i: (0,) * len(shape))
    widths = (2 * CV_CH, DN_QKV, DN_QK, LANES)
    return pl.pallas_call(
        _cd_in_kernel,
        grid=(b, nt),
        in_specs=[pl.BlockSpec((None, TM, d), lambda bi, i: (bi, i, 0)),
                  pl.BlockSpec((None, None, 6, d), _mod_index),
                  full((1, d)), full((d, n))],
        out_specs=[pl.BlockSpec((None, TM, w_), lambda bi, i: (bi, i, 0)) for w_ in widths],
        out_shape=[jax.ShapeDtypeStruct((b, l, w_), f32) for w_ in widths],
        compiler_params=_cparams(("arbitrary", "arbitrary")),
        name="cd_in_proj",
    )(x, modl, g, w)


CV_HALO = 16
CV_ROWS = 32


def _conformer_kernel(prev_ref, cur_ref, next_ref, w_ref, b_ref, lng_ref, lnb_ref, o_ref, ext_ref):
    i = pl.program_id(1)
    last = pl.num_programs(1) - 1

    def glu(z):
        return z[:, 0:CV_CH] * _sigmoid(z[:, CV_CH:2 * CV_CH])

    ext_ref[0:CV_HALO, :] = jnp.where(i > 0, glu(prev_ref[...]), 0.0)
    ext_ref[CV_HALO:CV_HALO + TM, :] = glu(cur_ref[...])
    ext_ref[CV_HALO + TM:2 * CV_HALO + TM, :] = jnp.where(i < last, glu(next_ref[...]), 0.0)
    w = w_ref[...]
    base = CV_HALO - CV_WIDTH // 2
    for rb in range(TM // CV_ROWS):
        acc = jnp.zeros((CV_ROWS, CV_CH), f32)
        for j in range(CV_WIDTH):
            acc = acc + ext_ref[pl.ds(rb * CV_ROWS + base + j, CV_ROWS), :] * w[j:j + 1, :]
        y = _layer_norm_rows(acc + b_ref[...], lng_ref[...], lnb_ref[...])
        o_ref[rb * CV_ROWS:(rb + 1) * CV_ROWS, :] = _silu(y).astype(bf16)


def _conformer(ag, w, bias, lng, lnb):
    b, l, c2 = ag.shape
    nt = l // TM - 1
    hb = TM // CV_HALO
    n_halo = l // CV_HALO
    full = lambda shape: pl.BlockSpec(shape, lambda bi, i: (0,) * len(shape))
    return pl.pallas_call(
        _conformer_kernel,
        grid=(b, nt),
        in_specs=[pl.BlockSpec((None, CV_HALO, c2), lambda bi, i: (bi, (i + 1) * hb - 1, 0)),
                  pl.BlockSpec((None, TM, c2), lambda bi, i: (bi, i + 1, 0)),
                  pl.BlockSpec((None, CV_HALO, c2), lambda bi, i: (bi, jnp.minimum((i + 2) * hb, n_halo - 1), 0)),
                  full((CV_WIDTH, CV_CH)), full((1, CV_CH)), full((1, CV_CH)), full((1, CV_CH))],
        out_specs=pl.BlockSpec((None, TM, CV_CH), lambda bi, i: (bi, i, 0)),
        out_shape=jax.ShapeDtypeStruct((b, nt * TM, CV_CH), bf16),
        scratch_shapes=[pltpu.VMEM((TM + 2 * CV_HALO, CV_CH), f32)],
        compiler_params=_cparams(("arbitrary", "arbitrary")),
        name="conformer",
    )(ag, ag, ag, w, bias, lng, lnb)


DN_HALO = 8


def _delta_feat_kernel(prev_ref, cur_ref, next_ref, ab_ref, w_ref, alog_ref, dtb_ref,
                       q_ref, k_ref, v_ref, g_ref, ext_ref):
    i = pl.program_id(1)
    last = pl.num_programs(1) - 1
    ext_ref[0:DN_HALO, :] = jnp.where(i > 1, prev_ref[...], 0.0)
    ext_ref[DN_HALO:DN_HALO + TM, :] = cur_ref[...]
    ext_ref[DN_HALO + TM:2 * DN_HALO + TM, :] = jnp.where((i > 0) & (i < last), next_ref[...], 0.0)
    w = w_ref[...]
    base = DN_HALO - DN_CONV // 2
    outs = (q_ref, k_ref, v_ref)
    for c in range(DN_QKV // LANES):
        cs = slice(c * LANES, (c + 1) * LANES)
        acc = jnp.zeros((TM, LANES), f32)
        for j in range(DN_CONV):
            acc = acc + ext_ref[pl.ds(base + j, TM), cs] * w[j:j + 1, cs]
        y = _silu(acc)
        part, h = divmod(c, DN_HEADS)
        if part < 2:
            y = y * lax.rsqrt(jnp.sum(y * y, axis=-1, keepdims=True) + EPS)
            if part == 0:
                y = y * (DN_DK ** -0.5)
        outs[part][:, h * LANES:(h + 1) * LANES] = y

    ab = ab_ref[...]
    g = -jnp.exp(alog_ref[...]) * (jnp.maximum(ab + dtb_ref[...], 0.0)
                                   + jnp.log(1.0 + jnp.exp(-jnp.abs(ab + dtb_ref[...]))))
    beta = _sigmoid(ab)
    lane = lax.broadcasted_iota(i32, ab.shape, 1)
    row = lax.broadcasted_iota(i32, (TM, TM), 0)
    col = lax.broadcasted_iota(i32, (TM, TM), 1)
    same = (row // DN_CHUNK) == (col // DN_CHUNK)
    for d in range(N_DIR):
        gd = g if d == 0 else pltpu.roll(g, LANES - DN_HEADS, 1)
        bd = pltpu.roll(beta, LANES - DN_HEADS * (d + 1), 1)
        tri = (same & ((col <= row) if d == 0 else (col >= row))).astype(f32)
        gc = jnp.dot(tri, jnp.where(lane < DN_HEADS, gd, 0.0), precision=lax.Precision.HIGHEST,
                     preferred_element_type=f32)
        g_ref[d] = jnp.where(lane < DN_HEADS, gc, bd)


def _delta_features(qkv, ab, w, alog, dtb):
    b, l, c = qkv.shape
    nt = l // TM
    hb = TM // DN_HALO
    n_halo = l // DN_HALO
    full = lambda shape: pl.BlockSpec(shape, lambda bi, i: (0,) * len(shape))
    return pl.pallas_call(
        _delta_feat_kernel,
        grid=(b, nt),
        in_specs=[pl.BlockSpec((None, DN_HALO, c), lambda bi, i: (bi, jnp.maximum(i * hb - 1, 0), 0)),
                  pl.BlockSpec((None, TM, c), lambda bi, i: (bi, i, 0)),
                  pl.BlockSpec((None, DN_HALO, c), lambda bi, i: (bi, jnp.minimum((i + 1) * hb, n_halo - 1), 0)),
                  pl.BlockSpec((None, TM, LANES), lambda bi, i: (bi, i, 0)),
                  full((DN_CONV, c)), full((1, LANES)), full((1, LANES))],
        out_specs=[pl.BlockSpec((None, TM, DN_QK), lambda bi, i: (bi, i, 0))] * 3
        + [pl.BlockSpec((N_DIR, None, TM, LANES), lambda bi, i: (0, bi, i, 0))],
        out_shape=[jax.ShapeDtypeStruct((b, l, DN_QK), f32)] * 3
        + [jax.ShapeDtypeStruct((N_DIR, b, l, LANES), f32)],
        scratch_shapes=[pltpu.VMEM((TM + 2 * DN_HALO, c), f32)],
        compiler_params=_cparams(("arbitrary", "arbitrary")),
        name="delta_features",
    )(qkv, qkv, qkv, ab, w, alog, dtb)


def _delta_scan_kernel(q_ref, k_ref, v_ref, g_ref, o_ref, s_ref):
    d = pl.program_id(1)
    c = pl.program_id(2)
    n = DN_CHUNK

    @pl.when(c == 0)
    def _():
        s_ref[...] = jnp.zeros(s_ref.shape, f32)

    row = lax.broadcasted_iota(i32, (n, n), 0)
    col = lax.broadcasted_iota(i32, (n, n), 1)
    order = (row - col) * (1 - 2 * d)
    incl = order >= 0
    strict = order > 0
    eye = row == col
    gates = g_ref[...]
    rows = lax.broadcasted_iota(i32, gates.shape, 0)
    g_last = jnp.sum(jnp.where(rows == (n - 1) * (1 - d), gates, 0.0), axis=0, keepdims=True)

    for h in range(DN_HEADS):
        cs = slice(h * LANES, (h + 1) * LANES)
        q, k, v = q_ref[:, cs], k_ref[:, cs], v_ref[:, cs]
        gc = gates[:, h:h + 1]
        beta = gates[:, DN_HEADS + h:DN_HEADS + h + 1]
        gl = g_last[:, h:h + 1]
        gc_row = jnp.sum(jnp.where(eye, gc, 0.0), axis=0, keepdims=True)
        decay = jnp.exp(jnp.where(incl, gc - gc_row, -jnp.inf))
        kk = _dot_nt(k, k)
        a = jnp.where(strict, -(beta * kk * decay), 0.0)
        qk = _dot_nt(q, k) * decay
        y = a
        p = a
        for _ in range(5):
            p = _dot(p, p)
            y = y + p + _dot(y, p)
        e_gc = jnp.exp(gc)
        rhs = jnp.concatenate([v * beta, k * (beta * e_gc)], axis=1)
        sol = rhs + _dot(y, rhs)
        u, w = sol[:, 0:LANES], sol[:, LANES:2 * LANES]
        s = s_ref[h]
        r = _dot(jnp.concatenate([w, q * e_gc], axis=0), s)
        v_new = u - r[0:n]
        o_ref[:, cs] = r[n:2 * n] + _dot(qk, v_new)
        s_ref[h] = s * jnp.exp(gl) + _dot_tn(k * jnp.exp(gl - gc), v_new)


def _delta_scan(q, k, v, gates):
    b, l, c = q.shape
    nc = l // DN_CHUNK
    ctx_chunks = TM // DN_CHUNK

    def chunk(d, ci):
        rev = jnp.where(ci < ctx_chunks, ctx_chunks - 1 - ci, nc + ctx_chunks - 1 - ci)
        return jnp.where(d == 0, ci, rev)

    row = pl.BlockSpec((None, DN_CHUNK, c), lambda bi, d, ci: (bi, chunk(d, ci), 0))
    return pl.pallas_call(
        _delta_scan_kernel,
        grid=(b, N_DIR, nc),
        in_specs=[row, row, row,
                  pl.BlockSpec((None, None, DN_CHUNK, LANES), lambda bi, d, ci: (d, bi, chunk(d, ci), 0))],
        out_specs=pl.BlockSpec((None, None, DN_CHUNK, c), lambda bi, d, ci: (d, bi, chunk(d, ci), 0)),
        out_shape=jax.ShapeDtypeStruct((N_DIR, b, l, c), f32),
        scratch_shapes=[pltpu.VMEM((DN_HEADS, DN_DK, LANES), f32)],
        compiler_params=_cparams(("arbitrary", "arbitrary", "arbitrary")),
        name="delta_scan",
    )(q, k, v, gates)


def _rope_tables(s, ctx_len):
    rows = s // GRID_W
    row = jnp.repeat(jnp.arange(rows, dtype=i32), GRID_W)
    col = jnp.tile(jnp.arange(GRID_W, dtype=i32), rows)
    freqs = ROPE_THETA ** (-jnp.arange(ROPE_PAIRS, dtype=f32) / ROPE_PAIRS)
    ang = jnp.stack([row, col], axis=-1).astype(f32)[..., None] * freqs
    cos = jnp.cos(ang)
    sin = jnp.sin(ang)
    cos_h = jnp.concatenate([cos[:, 0], cos[:, 0], cos[:, 1], cos[:, 1]], axis=-1)
    sin_h = jnp.concatenate([-sin[:, 0], sin[:, 0], -sin[:, 1], sin[:, 1]], axis=-1)
    cos_t = jnp.concatenate([jnp.ones((ctx_len, HEAD_DIM), f32), cos_h], axis=0)
    sin_t = jnp.concatenate([jnp.zeros((ctx_len, HEAD_DIM), f32), sin_h], axis=0)
    return jnp.tile(cos_t, (1, 2)), jnp.tile(sin_t, (1, 2))


def _pad_lanes(v, n=LANES):
    v = v.reshape(1, -1)
    return jnp.pad(v, ((0, 0), (0, n - v.shape[1])))


def kernel(x, c, ctx, c_ctx, mod_w, mod_b, norm1_g, norm2_g, ab_w_in, ab_q_norm, ab_k_norm, gm_ln_g, gm_ln_b, gm_w_s, gm_b_s, ab_w_out, cd_w_in, cv_dw_w, cv_dw_b, cv_ln_g, cv_ln_b, dn_conv_w, dn_a_log, dn_dt_bias, dn_o_norm, cd_w_out, router_w, router_b, moe_w_gate, moe_w_up, moe_w_down, final_norm_g):
    b, s, d = x.shape
    ctx_len = ctx.shape[1]
    assert ctx_len == TM and s % (2 * TM) == 0 and mod_w.shape[0] == 2
    l = ctx_len + s
    nt = l // TM

    n_rows = -(-(b + 1) // 8) * 8
    cc = jnp.concatenate([c, c_ctx[None, :], jnp.zeros((n_rows - b - 1, d), f32)], axis=0)
    mod = _modulation(cc, mod_w, mod_b)
    mod_lat = mod[:, :b].reshape(2, b, 1, 6, d)
    mod_ctx = jnp.broadcast_to(mod[:, b].reshape(2, 1, 1, 6, d), (2, b, 1, 6, d))
    modl = jnp.concatenate([mod_ctx, mod_lat], axis=2)

    xs = jnp.concatenate([ctx, x], axis=1)
    rw = jnp.pad(router_w, ((0, 0), (0, LANES - N_EXPERTS)))
    rb = router_b.reshape(N_EXPERTS, 1)
    wg = moe_w_gate.astype(bf16)
    wu = moe_w_up.astype(bf16)
    wd = moe_w_down.astype(bf16)

    cos_t, sin_t = _rope_tables(s, ctx_len)
    lane = jnp.arange(2 * LANES)
    seg = ((lane[:, None] // HEAD_DIM) == (lane[None, :] // HEAD_DIM)).astype(bf16) * (1.0 / HEAD_DIM)
    q, kd, vt, sg = _ab_in_proj(
        xs, modl[0], norm1_g[0:1], ab_w_in[0].astype(bf16), seg,
        jnp.tile(ab_q_norm[0], 2)[None, :], jnp.tile(ab_k_norm[0], 2)[None, :], cos_t, sin_t,
        gm_ln_g[0:1], gm_ln_b[0:1], gm_w_s[0].astype(bf16),
        jnp.broadcast_to(gm_b_s[0][:, :, None], (GM_GROUPS, GM_CHUNK, LANES)))
    att = _attention(q, kd, vt)
    x1, h1, idx, wts = _out_proj(att, sg, xs, modl[0], norm2_g[0:1], ab_w_out[0].astype(bf16), rw, rb,
                                 delta=False, lat_only=False)
    xs = _moe(h1, x1, idx, wts, modl[0], wg[0], wu[0], wd[0], nt, False).reshape(b, l, d)

    c1 = 2 * CV_CH
    c2 = c1 + DN_QKV
    c3 = c2 + 2 * N_DIR * DN_HEADS
    w_in = cd_w_in[0]
    w_cd = jnp.concatenate([w_in[:, 0:c2], w_in[:, c3:], w_in[:, c2:c3],
                            jnp.zeros((d, LANES - (c3 - c2)), f32)], axis=1).astype(bf16)
    ag, qkv, dg, ab = _cd_in_proj(xs, modl[1], norm1_g[1:2], w_cd)
    conv = _conformer(ag, cv_dw_w[0], cv_dw_b[0:1], cv_ln_g[0:1], cv_ln_b[0:1])
    qn, kn, vv, gates = _delta_features(qkv, ab, dn_conv_w[0], _pad_lanes(dn_a_log[0]), _pad_lanes(dn_dt_bias[0]))
    o = _delta_scan(qn, kn, vv, gates)
    x2, h2, idx, wts = _out_proj(conv, (o, dg, dn_o_norm[0:1]), xs, modl[1], norm2_g[1:2],
                                 cd_w_out[0].astype(bf16), rw, rb, delta=True, lat_only=True)
    out = _moe(h2, x2, idx, wts, modl[1], wg[1], wu[1], wd[1], nt - 1, True, final_norm_g[None, :])
    return out.reshape(b, s, d)
```

```python
import functools
import math

import jax
import jax.numpy as jnp
from jax import lax
from jax.experimental import pallas as pl
from jax.experimental.pallas import tpu as pltpu

f32 = jnp.float32
bf16 = jnp.bfloat16
i32 = jnp.int32

GRID_W = 64
EPS = 1e-6
ATT_HEADS = 8
ATT_KV_HEADS = 2
HEAD_DIM = 64
ROPE_PAIRS = HEAD_DIM // 4
ROPE_THETA = 10000.0
A_Q = ATT_HEADS * HEAD_DIM
A_KV = ATT_KV_HEADS * HEAD_DIM
GM_GROUPS = 4
GM_CHUNK = 128
GM_CH = 512
CV_CH = 512
CV_WIDTH = 31
DN_HEADS = 4
DN_DK = 128
DN_CONV = 5
DN_CHUNK = 64
N_DIR = 2
DN_QK = DN_HEADS * DN_DK
DN_QKV = 3 * DN_QK
N_EXPERTS = 16
N_GROUPS = 4
EXPERT_FF = 512

LANES = 128
TM = 256
V_ROWS = HEAD_DIM + 16
VMEM_LIMIT = 48 * 1024 * 1024


def _cparams(sem):
    return pltpu.CompilerParams(dimension_semantics=sem, vmem_limit_bytes=VMEM_LIMIT)


def _sigmoid(x):
    return 1.0 / (1.0 + jnp.exp(-x))


def _silu(x):
    return x * _sigmoid(x)


def _gelu(x):
    return x * (0.5 * (1.0 + jnp.tanh(math.sqrt(2.0 / math.pi) * (x + 0.044715 * (x * x * x)))))


def _rms_rows(x):
    return x * lax.rsqrt(jnp.mean(x * x, axis=-1, keepdims=True) + EPS)


def _layer_norm_rows(x, g, b):
    mu = jnp.mean(x, axis=-1, keepdims=True)
    xc = x - mu
    var = jnp.mean(xc * xc, axis=-1, keepdims=True)
    return xc * lax.rsqrt(var + EPS) * g + b


def _dot(a, b):
    return jnp.dot(a.astype(bf16), b.astype(bf16), preferred_element_type=f32)


def _dot_nt(a, b):
    return lax.dot_general(a.astype(bf16), b.astype(bf16), (((1,), (1,)), ((), ())), preferred_element_type=f32)


def _dot_tn(a, b):
    return lax.dot_general(a.astype(bf16), b.astype(bf16), (((0,), (0,)), ((), ())), preferred_element_type=f32)


def _mod_kernel(c_ref, w_ref, b_ref, o_ref):
    c = c_ref[...]
    o_ref[...] = jnp.dot(_silu(c), w_ref[...], precision=lax.Precision.HIGHEST,
                         preferred_element_type=f32) + b_ref[...]


def _modulation(cc, mod_w, mod_b):
    depth, d, n = mod_w.shape
    tn = 1024
    rows = cc.shape[0]
    return pl.pallas_call(
        _mod_kernel,
        grid=(depth, n // tn),
        in_specs=[pl.BlockSpec((rows, d), lambda l, j: (0, 0)),
                  pl.BlockSpec((None, d, tn), lambda l, j: (l, 0, j)),
                  pl.BlockSpec((None, 1, tn), lambda l, j: (l, 0, j))],
        out_specs=pl.BlockSpec((None, rows, tn), lambda l, j: (l, 0, j)),
        out_shape=jax.ShapeDtypeStruct((depth, rows, n), f32),
        compiler_params=_cparams(("arbitrary", "arbitrary")),
        name="modulation",
    )(cc, mod_w, mod_b.reshape(depth, 1, n))


def _mod_index(b, i):
    return (b, jnp.minimum(i, 1), 0, 0)


def _head_norm_rope(x, seg_mean, gain, cos, sin, scale):
    xx = x * x
    outs = []
    lane = lax.broadcasted_iota(i32, (x.shape[0], LANES), 1)
    first = (lane % 32) < 16
    for c0 in range(0, x.shape[1], 2 * LANES):
        w = min(2 * LANES, x.shape[1] - c0)
        ms = _dot(xx[:, c0:c0 + w], seg_mean[0:w, 0:w])
        xn = x[:, c0:c0 + w] * lax.rsqrt(ms + EPS)
        for c1 in range(0, w, LANES):
            y = xn[:, c1:c1 + LANES] * gain
            partner = jnp.where(first, pltpu.roll(y, LANES - 16, 1), pltpu.roll(y, 16, 1))
            outs.append((y * cos + partner * sin) * scale)
    return outs


def _ab_in_kernel(x_ref, mod_ref, g_ref, w_ref, seg_ref, qg_ref, kg_ref, cos_ref, sin_ref,
                  lng_ref, lnb_ref, ws_ref, bs_ref, q_ref, kd_ref, vt_ref, s_ref):
    mod = mod_ref[...]
    h = _rms_rows(x_ref[...]) * g_ref[...] * (1.0 + mod[1:2]) + mod[0:1]
    acc = _dot(h, w_ref[...])
    cos = cos_ref[...]
    sin = sin_ref[...]
    seg = seg_ref[...]

    q = _head_norm_rope(acc[:, 0:A_Q], seg, qg_ref[...], cos, sin, HEAD_DIM ** -0.5)
    for c, qc in enumerate(q):
        q_ref[:, c * LANES:(c + 1) * LANES] = qc.astype(bf16)

    (k,) = _head_norm_rope(acc[:, A_Q:A_Q + A_KV], seg, kg_ref[...], cos, sin, 1.0)
    lane = lax.broadcasted_iota(i32, k.shape, 1)
    swapped = pltpu.roll(k, HEAD_DIM, 1)
    kd_ref[0] = jnp.where(lane < HEAD_DIM, k, swapped).astype(bf16)
    kd_ref[1] = jnp.where(lane < HEAD_DIM, swapped, k).astype(bf16)

    vt = acc[:, A_Q + A_KV:A_Q + 2 * A_KV].T
    ones = jnp.ones((V_ROWS - HEAD_DIM, vt.shape[1]), bf16)
    for j in range(ATT_KV_HEADS):
        vt_ref[j, 0:HEAD_DIM, :] = vt[j * HEAD_DIM:(j + 1) * HEAD_DIM].astype(bf16)
        vt_ref[j, HEAD_DIM:V_ROWS, :] = ones

    c0 = A_Q + 2 * A_KV
    gu = _gelu(acc[:, c0:c0 + GM_CH])
    ln = _layer_norm_rows(_gelu(acc[:, c0 + GM_CH:c0 + 2 * GM_CH]), lng_ref[...], lnb_ref[...])
    for n in range(acc.shape[0] // GM_CHUNK):
        r = slice(n * GM_CHUNK, (n + 1) * GM_CHUNK)
        for g in range(GM_GROUPS):
            cs = slice(g * LANES, (g + 1) * LANES)
            mixed = _dot(ws_ref[g], ln[r, cs]) + bs_ref[g]
            s_ref[r, cs] = (gu[r, cs] * mixed).astype(bf16)


def _ab_in_proj(x, modl, g, w, seg, qg, kg, cos, sin, lng, lnb, ws, bs):
    b, l, d = x.shape
    nt = l // TM
    n = w.shape[1]
    full = lambda shape: pl.BlockSpec(shape, lambda bi, i: (0,) * len(shape))
    return pl.pallas_call(
        _ab_in_kernel,
        grid=(b, nt),
        in_specs=[pl.BlockSpec((None, TM, d), lambda bi, i: (bi, i, 0)),
                  pl.BlockSpec((None, None, 6, d), _mod_index),
                  full((1, d)), full((d, n)), full((2 * LANES, 2 * LANES)),
                  full((1, LANES)), full((1, LANES)),
                  pl.BlockSpec((TM, LANES), lambda bi, i: (i, 0)),
                  pl.BlockSpec((TM, LANES), lambda bi, i: (i, 0)),
                  full((1, GM_CH)), full((1, GM_CH)),
                  full((GM_GROUPS, GM_CHUNK, GM_CHUNK)), full((GM_GROUPS, GM_CHUNK, LANES))],
        out_specs=[pl.BlockSpec((None, TM, A_Q), lambda bi, i: (bi, i, 0)),
                   pl.BlockSpec((None, ATT_KV_HEADS, TM, LANES), lambda bi, i: (bi, 0, i, 0)),
                   pl.BlockSpec((None, ATT_KV_HEADS, None, V_ROWS, TM), lambda bi, i: (bi, 0, i, 0, 0)),
                   pl.BlockSpec((None, TM, GM_CH), lambda bi, i: (bi, i, 0))],
        out_shape=[jax.ShapeDtypeStruct((b, l, A_Q), bf16),
                   jax.ShapeDtypeStruct((b, ATT_KV_HEADS, l, LANES), bf16),
                   jax.ShapeDtypeStruct((b, ATT_KV_HEADS, nt, V_ROWS, TM), bf16),
                   jax.ShapeDtypeStruct((b, l, GM_CH), bf16)],
        compiler_params=_cparams(("arbitrary", "arbitrary")),
        name="ab_in_proj",
    )(x, modl, g, w, seg, qg, kg, cos, sin, lng, lnb, ws, bs)


def _attn_kernel(q_ref, k_ref, vt_ref, o_ref, acc_ref, m_ref, *, n_pairs):
    i = pl.program_id(2)
    q = q_ref[...]
    lane = lax.broadcasted_iota(i32, (TM, LANES), 1)
    zero = jnp.zeros((TM, LANES), bf16)
    pieces = []
    for p in range(2):
        qp = q[:, p * LANES:(p + 1) * LANES]
        pieces.append(jnp.where(lane < HEAD_DIM, qp, zero))
        pieces.append(jnp.where(lane < HEAD_DIM, zero, qp))
    qs = jnp.concatenate(pieces, axis=0)
    nq = qs.shape[0]
    m_ref[...] = jnp.full((1, nq), -jnp.inf, f32)
    acc_ref[...] = jnp.zeros((V_ROWS, nq), f32)

    def block(kb, vb):
        st = _dot_nt(kb, qs)
        m_old = m_ref[...]
        m_new = jnp.maximum(m_old, jnp.max(st, axis=0, keepdims=True))
        p = jnp.exp(st - m_new).astype(bf16)
        acc_ref[...] = acc_ref[...] * jnp.exp(m_old - m_new) + jnp.dot(vb, p, preferred_element_type=f32)
        m_ref[...] = m_new

    block(k_ref[0:TM, :], vt_ref[0])

    @pl.when(i > 0)
    def _():
        def body(t, carry):
            start = pl.multiple_of(TM + t * (2 * TM), TM)
            vb = jnp.concatenate([vt_ref[1 + 2 * t], vt_ref[2 + 2 * t]], axis=1)
            block(k_ref[pl.ds(start, 2 * TM), :], vb)
            return carry
        lax.fori_loop(0, n_pairs, body, 0)

    acc = acc_ref[...]
    o = acc[0:HEAD_DIM] / acc[HEAD_DIM:HEAD_DIM + 1]
    for p in range(2):
        pair = jnp.concatenate([o[:, (2 * p) * TM:(2 * p + 1) * TM],
                                o[:, (2 * p + 1) * TM:(2 * p + 2) * TM]], axis=0)
        o_ref[:, p * LANES:(p + 1) * LANES] = pair.T.astype(bf16)


def _attention(q, kd, vt):
    b, l, _ = q.shape
    nt = l // TM
    group_w = A_Q // ATT_KV_HEADS
    assert (l - TM) % (2 * TM) == 0
    return pl.pallas_call(
        functools.partial(_attn_kernel, n_pairs=(l - TM) // (2 * TM)),
        grid=(b, ATT_KV_HEADS, nt),
        in_specs=[pl.BlockSpec((None, TM, group_w), lambda bi, j, i: (bi, i, j)),
                  pl.BlockSpec((None, None, l, LANES), lambda bi, j, i: (bi, j, 0, 0)),
                  pl.BlockSpec((None, None, nt, V_ROWS, TM), lambda bi, j, i: (bi, j, 0, 0, 0))],
        out_specs=pl.BlockSpec((None, TM, group_w), lambda bi, j, i: (bi, i, j)),
        out_shape=jax.ShapeDtypeStruct((b, l, A_Q), bf16),
        scratch_shapes=[pltpu.VMEM((V_ROWS, 4 * TM), f32), pltpu.VMEM((1, 4 * TM), f32)],
        compiler_params=_cparams(("arbitrary", "arbitrary", "arbitrary")),
        name="attention",
    )(q, kd, vt)


def _route_rows(scores, sel):
    s = [sel[e:e + 1] for e in range(N_EXPERTS)]
    sc = [scores[e:e + 1] for e in range(N_EXPERTS)]
    per = N_EXPERTS // N_GROUPS
    gs = []
    for g in range(N_GROUPS):
        a, b, c, d = s[per * g:per * g + per]
        hi1, lo1 = jnp.maximum(a, b), jnp.minimum(a, b)
        hi2, lo2 = jnp.maximum(c, d), jnp.minimum(c, d)
        gs.append(jnp.maximum(hi1, hi2) + jnp.maximum(jnp.minimum(hi1, hi2), jnp.maximum(lo1, lo2)))
    best = jnp.zeros(gs[0].shape, i32)
    best_v = gs[0]
    for g in range(1, N_GROUPS):
        better = gs[g] > best_v
        best = jnp.where(better, g, best)
        best_v = jnp.where(better, gs[g], best_v)
    v, w = [], []
    for j in range(per):
        vj, wj = s[j], sc[j]
        for g in range(1, N_GROUPS):
            vj = jnp.where(best == g, s[per * g + j], vj)
            wj = jnp.where(best == g, sc[per * g + j], wj)
        v.append(vj)
        w.append(wj)
    i1 = jnp.zeros(best.shape, i32)
    m1, w1 = v[0], w[0]
    for j in range(1, per):
        better = v[j] > m1
        i1 = jnp.where(better, j, i1)
        m1 = jnp.where(better, v[j], m1)
        w1 = jnp.where(better, w[j], w1)
    i2 = jnp.zeros(best.shape, i32)
    m2 = jnp.full(m1.shape, -jnp.inf, f32)
    w2 = jnp.zeros(m1.shape, f32)
    for j in range(per):
        cand = jnp.where(i1 == j, -jnp.inf, v[j])
        better = cand > m2
        i2 = jnp.where(better, j, i2)
        m2 = jnp.where(better, cand, m2)
        w2 = jnp.where(better, w[j], w2)
    tot = w1 + w2
    return best * per + i1, best * per + i2, w1 / tot, w2 / tot


def _out_kernel(*refs, delta):
    if delta:
        (a1_ref, of_ref, ob_ref, dg_ref, on_ref, x_ref, mod_ref, g_ref, w_ref, rw_ref, rb_ref,
         xo_ref, h_ref, idx_ref, wt_ref) = refs
        o = of_ref[...] + ob_ref[...]
        dg = dg_ref[...]
        parts = []
        for h in range(DN_HEADS):
            cs = slice(h * LANES, (h + 1) * LANES)
            parts.append(_rms_rows(o[:, cs]) * on_ref[...] * _silu(dg[:, cs]))
        a2 = jnp.concatenate(parts, axis=1)
    else:
        (a1_ref, a2_ref, x_ref, mod_ref, g_ref, w_ref, rw_ref, rb_ref,
         xo_ref, h_ref, idx_ref, wt_ref) = refs
        a2 = a2_ref[...]
    half = a1_ref.shape[-1]
    y = _dot(a1_ref[...], w_ref[0:half, :]) + _dot(a2, w_ref[half:2 * half, :])
    mod = mod_ref[...]
    xn = x_ref[...] + mod[2:3] * y
    xo_ref[...] = xn
    h2 = _rms_rows(xn) * g_ref[...] * (1.0 + mod[4:5]) + mod[3:4]
    h_ref[...] = h2
    logits = jnp.dot(h2, rw_ref[...], precision=lax.Precision.HIGHEST, preferred_element_type=f32)
    scores = _sigmoid(logits.T[0:N_EXPERTS])
    e1, e2, w1, w2 = _route_rows(scores, scores + rb_ref[...])
    idx_ref[0:1, :] = e1
    idx_ref[1:2, :] = e2
    wt_ref[0:1, :] = w1
    wt_ref[1:2, :] = w2


def _out_proj(a1, a2s, x, modl, g, w, rw, rb, *, delta, lat_only):
    b, l, d = x.shape
    nt = l // TM
    off = 1 if lat_only else 0
    nto = nt - off
    half = a1.shape[-1]
    full = lambda shape: pl.BlockSpec(shape, lambda bi, i: (0,) * len(shape))
    row_l = lambda w_: pl.BlockSpec((None, TM, w_), lambda bi, i: (bi, i + off, 0))
    if delta:
        a2_specs = [row_l(half), row_l(half), row_l(half), full((1, LANES))]
        a2_args = list(a2s)
        a1_spec = pl.BlockSpec((None, TM, half), lambda bi, i: (bi, i, 0))
    else:
        a2_specs = [row_l(half)]
        a2_args = [a2s]
        a1_spec = row_l(half)
    t_out = b * nto * TM
    flat = lambda bi, i: (bi * nto + i, 0)
    return pl.pallas_call(
        functools.partial(_out_kernel, delta=delta),
        grid=(b, nto),
        in_specs=[a1_spec] + a2_specs + [
            row_l(d),
            pl.BlockSpec((None, None, 6, d), lambda bi, i: (bi, jnp.minimum(i + off, 1), 0, 0)),
            full((1, d)), full((2 * half, d)), full((d, LANES)), full((N_EXPERTS, 1))],
        out_specs=[pl.BlockSpec((TM, d), flat), pl.BlockSpec((TM, d), flat),
                   pl.BlockSpec((2, TM), lambda bi, i: (0, bi * nto + i)),
                   pl.BlockSpec((2, TM), lambda bi, i: (0, bi * nto + i))],
        out_shape=[jax.ShapeDtypeStruct((t_out, d), f32), jax.ShapeDtypeStruct((t_out, d), f32),
                   jax.ShapeDtypeStruct((2, t_out), i32), jax.ShapeDtypeStruct((2, t_out), f32)],
        compiler_params=_cparams(("arbitrary", "arbitrary")),
        name="out_proj_delta" if delta else "out_proj",
    )(a1, *a2_args, x, modl, g, w, rw, rb)


MOE_GROUP = 8
MOE_BUF = 2 * TM + N_EXPERTS * MOE_GROUP


def _dispatch_plan(idx):
    t = idx.shape[1]
    nt = t // TM
    e = idx.reshape(2, nt, TM).transpose(1, 0, 2).reshape(nt, 2 * TM)
    onehot = (e[:, :, None] == jnp.arange(N_EXPERTS, dtype=i32)).astype(i32)
    lrank = jnp.cumsum(onehot, axis=1) - onehot
    cnt = jnp.sum(onehot, axis=1)
    cnt_g = (cnt + MOE_GROUP - 1) // MOE_GROUP * MOE_GROUP
    boff = jnp.cumsum(cnt_g, axis=1) - cnt_g
    bufpos = jnp.sum((boff[:, None, :] + lrank) * onehot, axis=2)
    region = jnp.sum(cnt_g, axis=0)
    padded = (region + TM - 1) // TM * TM
    ends = jnp.cumsum(padded)
    base = ends - padded
    gstart = base[None, :] + jnp.cumsum(cnt_g, axis=0) - cnt_g
    n_tiles = -(-(2 * t + nt * N_EXPERTS * (MOE_GROUP - 1)) // TM) + N_EXPERTS
    n_used = (ends[-1] // TM).astype(i32)
    tiles = jnp.arange(n_tiles, dtype=i32)
    tile_e = jnp.sum((tiles[:, None] * TM >= ends[None, :]).astype(i32), axis=1)
    last_e = jnp.sum(((n_used - 1) * TM >= ends).astype(i32))
    tile_e = jnp.where(tiles < n_used, tile_e, last_e)
    runs = (gstart.reshape(-1), boff.reshape(-1), (cnt_g // MOE_GROUP).reshape(-1))
    gaps = (jnp.concatenate([base + region, ends[-1:]]),
            jnp.concatenate([(padded - region) // MOE_GROUP, n_tiles - n_used.reshape(1)]))
    return runs, gaps, bufpos.reshape(nt, 2, TM), tile_e, n_used.reshape(1), n_tiles


def _run_copies(runs, tile, buf, hbm, sem, *, to_hbm, start):
    gs_ref, bo_ref, ng_ref = runs
    for e in range(N_EXPERTS):
        g0 = gs_ref[tile * N_EXPERTS + e]
        b0 = bo_ref[tile * N_EXPERTS + e]

        def body(c, carry, g0=g0, b0=b0):
            v = buf.at[pl.ds(pl.multiple_of(b0 + c * MOE_GROUP, MOE_GROUP), MOE_GROUP)]
            h = hbm.at[pl.ds(pl.multiple_of(g0 + c * MOE_GROUP, MOE_GROUP), MOE_GROUP)]
            cp = pltpu.make_async_copy(v, h, sem) if to_hbm else pltpu.make_async_copy(h, v, sem)
            if start:
                cp.start()
            else:
                cp.wait()
            return carry
        lax.fori_loop(0, ng_ref[tile * N_EXPERTS + e], body, 0)


def _moe_dispatch_kernel(gs_ref, bo_ref, ng_ref, gap0_ref, gapn_ref, h_ref, bp_ref, xs_hbm, buf, zbuf, sem, zsem):
    runs = (gs_ref, bo_ref, ng_ref)
    i = pl.program_id(0)
    n = pl.num_programs(0)
    slot = i % 2
    bp = bp_ref[...]
    j = lax.broadcasted_iota(i32, (MOE_BUF, TM), 0)
    sel = jnp.where((j == bp[0:1]) | (j == bp[1:2]), 1.0, 0.0).astype(bf16)
    buf[slot] = jnp.dot(sel, h_ref[...].astype(bf16), preferred_element_type=f32)
    _run_copies(runs, i, buf.at[slot], xs_hbm, sem.at[slot], to_hbm=True, start=True)

    @pl.when(i > 0)
    def _():
        _run_copies(runs, i - 1, buf.at[1 - slot], xs_hbm, sem.at[1 - slot], to_hbm=True, start=False)

    @pl.when(i == n - 1)
    def _():
        _run_copies(runs, i, buf.at[slot], xs_hbm, sem.at[slot], to_hbm=True, start=False)
        zbuf[...] = jnp.zeros(zbuf.shape, f32)
        for start in (True, False):
            for e in range(N_EXPERTS + 1):
                rows = MOE_GROUP if e < N_EXPERTS else TM

                def body(c, carry, e=e, start=start, rows=rows):
                    dst = xs_hbm.at[pl.ds(pl.multiple_of(gap0_ref[e] + c * rows, MOE_GROUP), rows)]
                    cp = pltpu.make_async_copy(zbuf.at[pl.ds(0, rows)], dst, zsem)
                    if start:
                        cp.start()
                    else:
                        cp.wait()
                    return carry
                lax.fori_loop(0, gapn_ref[e], body, 0)


def _moe_dispatch(h, runs, gaps, bufpos, n_tiles):
    t, d = h.shape
    grid_spec = pltpu.PrefetchScalarGridSpec(
        num_scalar_prefetch=5,
        grid=(t // TM,),
        in_specs=[pl.BlockSpec((TM, d), lambda i, *_: (i, 0)),
                  pl.BlockSpec((None, 2, TM), lambda i, *_: (i, 0, 0))],
        out_specs=pl.BlockSpec(memory_space=pl.ANY),
        scratch_shapes=[pltpu.VMEM((2, MOE_BUF, d), f32), pltpu.VMEM((TM, d), f32),
                        pltpu.SemaphoreType.DMA((2,)), pltpu.SemaphoreType.DMA],
    )
    return pl.pallas_call(
        _moe_dispatch_kernel,
        grid_spec=grid_spec,
        out_shape=jax.ShapeDtypeStruct((n_tiles * TM, d), f32),
        compiler_params=_cparams(("arbitrary",)),
        name="moe_dispatch",
    )(*runs, *gaps, h, bufpos)


def _moe_ffn_kernel(te_ref, nu_ref, x_ref, wg_ref, wu_ref, wd_ref, y_ref):
    @pl.when(pl.program_id(0) < nu_ref[0])
    def _():
        x = x_ref[...]
        a = _silu(_dot(x, wg_ref[...])) * _dot(x, wu_ref[...])
        y_ref[...] = _dot(a, wd_ref[...])

    @pl.when(pl.program_id(0) >= nu_ref[0])
    def _():
        y_ref[...] = jnp.zeros(y_ref.shape, f32)


def _moe_ffn(xs, tile_e, n_used, wg, wu, wd):
    p, d = xs.shape
    ff = wg.shape[2]
    row = lambda i, te, nu: (jnp.minimum(i, nu[0] - 1), 0)
    grid_spec = pltpu.PrefetchScalarGridSpec(
        num_scalar_prefetch=2,
        grid=(p // TM,),
        in_specs=[pl.BlockSpec((TM, d), row),
                  pl.BlockSpec((None, d, ff), lambda i, te, nu: (te[i], 0, 0)),
                  pl.BlockSpec((None, d, ff), lambda i, te, nu: (te[i], 0, 0)),
                  pl.BlockSpec((None, ff, d), lambda i, te, nu: (te[i], 0, 0))],
        out_specs=pl.BlockSpec((TM, d), lambda i, te, nu: (i, 0)),
    )
    return pl.pallas_call(
        _moe_ffn_kernel,
        grid_spec=grid_spec,
        out_shape=jax.ShapeDtypeStruct((p, d), f32),
        compiler_params=_cparams(("arbitrary",)),
        name="moe_ffn",
    )(tile_e, n_used, xs, wg, wu, wd)


def _moe_combine_kernel(gs_ref, bo_ref, ng_ref, y_hbm, x_ref, bp_ref, wt_ref, mod_ref, *rest, final):
    if final:
        fg_ref, o_ref, ybuf, sem = rest
    else:
        o_ref, ybuf, sem = rest
    runs = (gs_ref, bo_ref, ng_ref)
    i = pl.program_id(0)
    n = pl.num_programs(0)
    slot = i % 2

    @pl.when(i == 0)
    def _():
        ybuf[...] = jnp.zeros(ybuf.shape, f32)
        _run_copies(runs, 0, ybuf.at[0], y_hbm, sem.at[0], to_hbm=False, start=True)

    @pl.when(i + 1 < n)
    def _():
        _run_copies(runs, i + 1, ybuf.at[1 - slot], y_hbm, sem.at[1 - slot], to_hbm=False, start=True)

    _run_copies(runs, i, ybuf.at[slot], y_hbm, sem.at[slot], to_hbm=False, start=False)
    bp = bp_ref[...]
    w = wt_ref[...]
    lane = lax.broadcasted_iota(i32, (TM, MOE_BUF), 1)
    sel = jnp.where(lane == bp[:, 0:1], w[:, 0:1], 0.0) + jnp.where(lane == bp[:, 1:2], w[:, 1:2], 0.0)
    out = x_ref[...] + mod_ref[5:6] * _dot(sel, ybuf[slot])
    if final:
        out = _rms_rows(out) * fg_ref[...]
    o_ref[...] = out


def _moe_combine(y, runs, bufpos, x, wts, modl, tiles_per_batch, lat_only, final_g=None):
    t, d = x.shape
    n = t // TM
    off = 1 if lat_only else 0
    final = final_g is not None
    in_specs = [pl.BlockSpec(memory_space=pl.ANY),
                pl.BlockSpec((TM, d), lambda i, *_: (i, 0)),
                pl.BlockSpec((TM, 2), lambda i, *_: (i, 0)),
                pl.BlockSpec((TM, 2), lambda i, *_: (i, 0)),
                pl.BlockSpec((None, None, 6, d),
                             lambda i, *_: (i // tiles_per_batch, jnp.minimum(i % tiles_per_batch + off, 1), 0, 0))]
    args = [y, x, bufpos.transpose(0, 2, 1).reshape(t, 2), wts.T, modl]
    if final:
        in_specs.append(pl.BlockSpec((1, d), lambda i, *_: (0, 0)))
        args.append(final_g)
    grid_spec = pltpu.PrefetchScalarGridSpec(
        num_scalar_prefetch=3,
        grid=(n,),
        in_specs=in_specs,
        out_specs=pl.BlockSpec((TM, d), lambda i, *_: (i, 0)),
        scratch_shapes=[pltpu.VMEM((2, MOE_BUF, d), f32), pltpu.SemaphoreType.DMA((2,))],
    )
    return pl.pallas_call(
        functools.partial(_moe_combine_kernel, final=final),
        grid_spec=grid_spec,
        out_shape=jax.ShapeDtypeStruct((t, d), f32),
        compiler_params=_cparams(("arbitrary",)),
        name="moe_combine_final" if final else "moe_combine",
    )(*runs, *args)


def _moe(h, x, idx, wts, modl, wg, wu, wd, tiles_per_batch, lat_only, final_g=None):
    runs, gaps, bufpos, tile_e, n_used, n_tiles = _dispatch_plan(idx)
    xs = _moe_dispatch(h, runs, gaps, bufpos, n_tiles)
    y = _moe_ffn(xs, tile_e, n_used, wg, wu, wd)
    return _moe_combine(y, runs, bufpos, x, wts, modl, tiles_per_batch, lat_only, final_g)


def _cd_in_kernel(x_ref, mod_ref, g_ref, w_ref, ag_ref, qkv_ref, dg_ref, ab_ref):
    mod = mod_ref[...]
    h = _rms_rows(x_ref[...]) * g_ref[...] * (1.0 + mod[1:2]) + mod[0:1]
    acc = _dot(h, w_ref[...])
    c1 = 2 * CV_CH
    c2 = c1 + DN_QKV
    c3 = c2 + DN_QK
    ag_ref[...] = acc[:, 0:c1]
    qkv_ref[...] = acc[:, c1:c2]
    dg_ref[...] = acc[:, c2:c3]
    ab_ref[...] = acc[:, c3:c3 + LANES]


def _cd_in_proj(x, modl, g, w):
    b, l, d = x.shape
    nt = l // TM
    n = w.shape[1]
    full = lambda shape: pl.BlockSpec(shape, lambda bi, i: (0,) * len(shape))
    widths = (2 * CV_CH, DN_QKV, DN_QK, LANES)
    return pl.pallas_call(
        _cd_in_kernel,
        grid=(b, nt),
        in_specs=[pl.BlockSpec((None, TM, d), lambda bi, i: (bi, i, 0)),
                  pl.BlockSpec((None, None, 6, d), _mod_index),
                  full((1, d)), full((d, n))],
        out_specs=[pl.BlockSpec((None, TM, w_), lambda bi, i: (bi, i, 0)) for w_ in widths],
        out_shape=[jax.ShapeDtypeStruct((b, l, w_), f32) for w_ in widths],
        compiler_params=_cparams(("arbitrary", "arbitrary")),
        name="cd_in_proj",
    )(x, modl, g, w)


CV_HALO = 16
CV_ROWS = 32


def _conformer_kernel(prev_ref, cur_ref, next_ref, w_ref, b_ref, lng_ref, lnb_ref, o_ref, ext_ref):
    i = pl.program_id(1)
    last = pl.num_programs(1) - 1

    def glu(z):
        return z[:, 0:CV_CH] * _sigmoid(z[:, CV_CH:2 * CV_CH])

    ext_ref[0:CV_HALO, :] = jnp.where(i > 0, glu(prev_ref[...]), 0.0)
    ext_ref[CV_HALO:CV_HALO + TM, :] = glu(cur_ref[...])
    ext_ref[CV_HALO + TM:2 * CV_HALO + TM, :] = jnp.where(i < last, glu(next_ref[...]), 0.0)
    w = w_ref[...]
    base = CV_HALO - CV_WIDTH // 2
    for rb in range(TM // CV_ROWS):
        acc = jnp.zeros((CV_ROWS, CV_CH), f32)
        for j in range(CV_WIDTH):
            acc = acc + ext_ref[pl.ds(rb * CV_ROWS + base + j, CV_ROWS), :] * w[j:j + 1, :]
        y = _layer_norm_rows(acc + b_ref[...], lng_ref[...], lnb_ref[...])
        o_ref[rb * CV_ROWS:(rb + 1) * CV_ROWS, :] = _silu(y).astype(bf16)


def _conformer(ag, w, bias, lng, lnb):
    b, l, c2 = ag.shape
    nt = l // TM - 1
    hb = TM // CV_HALO
    n_halo = l // CV_HALO
    full = lambda shape: pl.BlockSpec(shape, lambda bi, i: (0,) * len(shape))
    return pl.pallas_call(
        _conformer_kernel,
        grid=(b, nt),
        in_specs=[pl.BlockSpec((None, CV_HALO, c2), lambda bi, i: (bi, (i + 1) * hb - 1, 0)),
                  pl.BlockSpec((None, TM, c2), lambda bi, i: (bi, i + 1, 0)),
                  pl.BlockSpec((None, CV_HALO, c2), lambda bi, i: (bi, jnp.minimum((i + 2) * hb, n_halo - 1), 0)),
                  full((CV_WIDTH, CV_CH)), full((1, CV_CH)), full((1, CV_CH)), full((1, CV_CH))],
        out_specs=pl.BlockSpec((None, TM, CV_CH), lambda bi, i: (bi, i, 0)),
        out_shape=jax.ShapeDtypeStruct((b, nt * TM, CV_CH), bf16),
        scratch_shapes=[pltpu.VMEM((TM + 2 * CV_HALO, CV_CH), f32)],
        compiler_params=_cparams(("arbitrary", "arbitrary")),
        name="conformer",
    )(ag, ag, ag, w, bias, lng, lnb)


DN_HALO = 8


def _delta_feat_kernel(prev_ref, cur_ref, next_ref, ab_ref, w_ref, alog_ref, dtb_ref,
                       q_ref, k_ref, v_ref, g_ref, ext_ref):
    i = pl.program_id(1)
    last = pl.num_programs(1) - 1
    ext_ref[0:DN_HALO, :] = jnp.where(i > 1, prev_ref[...], 0.0)
    ext_ref[DN_HALO:DN_HALO + TM, :] = cur_ref[...]
    ext_ref[DN_HALO + TM:2 * DN_HALO + TM, :] = jnp.where((i > 0) & (i < last), next_ref[...], 0.0)
    w = w_ref[...]
    base = DN_HALO - DN_CONV // 2
    outs = (q_ref, k_ref, v_ref)
    for c in range(DN_QKV // LANES):
        cs = slice(c * LANES, (c + 1) * LANES)
        acc = jnp.zeros((TM, LANES), f32)
        for j in range(DN_CONV):
            acc = acc + ext_ref[pl.ds(base + j, TM), cs] * w[j:j + 1, cs]
        y = _silu(acc)
        part, h = divmod(c, DN_HEADS)
        if part < 2:
            y = y * lax.rsqrt(jnp.sum(y * y, axis=-1, keepdims=True) + EPS)
            if part == 0:
                y = y * (DN_DK ** -0.5)
        outs[part][:, h * LANES:(h + 1) * LANES] = y

    ab = ab_ref[...]
    g = -jnp.exp(alog_ref[...]) * (jnp.maximum(ab + dtb_ref[...], 0.0)
                                   + jnp.log(1.0 + jnp.exp(-jnp.abs(ab + dtb_ref[...]))))
    beta = _sigmoid(ab)
    lane = lax.broadcasted_iota(i32, ab.shape, 1)
    row = lax.broadcasted_iota(i32, (TM, TM), 0)
    col = lax.broadcasted_iota(i32, (TM, TM), 1)
    same = (row // DN_CHUNK) == (col // DN_CHUNK)
    for d in range(N_DIR):
        gd = g if d == 0 else pltpu.roll(g, LANES - DN_HEADS, 1)
        bd = pltpu.roll(beta, LANES - DN_HEADS * (d + 1), 1)
        tri = (same & ((col <= row) if d == 0 else (col >= row))).astype(f32)
        gc = jnp.dot(tri, jnp.where(lane < DN_HEADS, gd, 0.0), precision=lax.Precision.HIGHEST,
                     preferred_element_type=f32)
        g_ref[d] = jnp.where(lane < DN_HEADS, gc, bd)


def _delta_features(qkv, ab, w, alog, dtb):
    b, l, c = qkv.shape
    nt = l // TM
    hb = TM // DN_HALO
    n_halo = l // DN_HALO
    full = lambda shape: pl.BlockSpec(shape, lambda bi, i: (0,) * len(shape))
    return pl.pallas_call(
        _delta_feat_kernel,
        grid=(b, nt),
        in_specs=[pl.BlockSpec((None, DN_HALO, c), lambda bi, i: (bi, jnp.maximum(i * hb - 1, 0), 0)),
                  pl.BlockSpec((None, TM, c), lambda bi, i: (bi, i, 0)),
                  pl.BlockSpec((None, DN_HALO, c), lambda bi, i: (bi, jnp.minimum((i + 1) * hb, n_halo - 1), 0)),
                  pl.BlockSpec((None, TM, LANES), lambda bi, i: (bi, i, 0)),
                  full((DN_CONV, c)), full((1, LANES)), full((1, LANES))],
        out_specs=[pl.BlockSpec((None, TM, DN_QK), lambda bi, i: (bi, i, 0))] * 3
        + [pl.BlockSpec((N_DIR, None, TM, LANES), lambda bi, i: (0, bi, i, 0))],
        out_shape=[jax.ShapeDtypeStruct((b, l, DN_QK), f32)] * 3
        + [jax.ShapeDtypeStruct((N_DIR, b, l, LANES), f32)],
        scratch_shapes=[pltpu.VMEM((TM + 2 * DN_HALO, c), f32)],
        compiler_params=_cparams(("arbitrary", "arbitrary")),
        name="delta_features",
    )(qkv, qkv, qkv, ab, w, alog, dtb)


DN_PRE_CHUNKS = 2
DN_EG_ROWS = 8


def _delta_pre_kernel(q_ref, k_ref, v_ref, g_ref, u_ref, w_ref, qe_ref, kd_ref, qk_ref, eg_ref):
    n = DN_CHUNK
    row = lax.broadcasted_iota(i32, (n, LANES), 0)
    col = lax.broadcasted_iota(i32, (n, LANES), 1)
    eye = row == col
    valid = col < n
    zpad = jnp.zeros((LANES - n, LANES), f32)
    sh = [(s, h) for s in range(DN_PRE_CHUNKS) for h in range(DN_HEADS)]
    rsl = lambda s: slice(s * n, (s + 1) * n)
    csl = lambda h: slice(h * LANES, (h + 1) * LANES)
    q = [q_ref[rsl(s), csl(h)] for s, h in sh]
    k = [k_ref[rsl(s), csl(h)] for s, h in sh]
    v = [v_ref[rsl(s), csl(h)] for s, h in sh]
    k_pad = [jnp.concatenate([ki, zpad], axis=0) for ki in k]
    kk = [_dot_nt(ki, kp) for ki, kp in zip(k, k_pad)]
    qk = [_dot_nt(qi, kp) for qi, kp in zip(q, k_pad)]
    prob = [(i, d) for i in range(len(sh)) for d in range(N_DIR)]
    gc, beta, gl, decay, a = [], [], [], [], []
    for i, d in prob:
        s, h = sh[i]
        gates = g_ref[d, rsl(s), :]
        gci = gates[:, h:h + 1]
        bi = gates[:, DN_HEADS + h:DN_HEADS + h + 1]
        last = n - 1 if d == 0 else 0
        incl = ((col <= row) if d == 0 else (col >= row)) & valid
        strict = ((col < row) if d == 0 else (col > row)) & valid
        gc_row = jnp.sum(jnp.where(eye, gci, 0.0), axis=0, keepdims=True)
        di = jnp.exp(jnp.where(incl, gci - gc_row, -jnp.inf))
        gc.append(gci)
        beta.append(bi)
        gl.append(gates[last:last + 1, h:h + 1])
        decay.append(di)
        a.append(jnp.where(strict, -(bi * kk[i] * di), 0.0))
    y = list(a)
    p = list(a)
    for _ in range(5):
        p = [_dot(pi[:, 0:n], pi) for pi in p]
        yp = [_dot(yi[:, 0:n], pi) for yi, pi in zip(y, p)]
        y = [yi + pi + ypi for yi, pi, ypi in zip(y, p, yp)]
    e_gc = [jnp.exp(g) for g in gc]
    rhs = [jnp.concatenate([v[i] * beta[j], k[i] * (beta[j] * e_gc[j])], axis=1) for j, (i, d) in enumerate(prob)]
    sol = [r + _dot(yi[:, 0:n], r) for yi, r in zip(y, rhs)]
    for j, (i, d) in enumerate(prob):
        s, h = sh[i]
        rs, cs = rsl(s), csl(h)
        u_ref[d, rs, cs] = sol[j][:, 0:LANES]
        w_ref[d, rs, cs] = sol[j][:, LANES:2 * LANES].astype(bf16)
        qe_ref[d, rs, cs] = (q[i] * e_gc[j]).astype(bf16)
        kd_ref[d, rs, cs] = (k[i] * jnp.exp(gl[j] - gc[j])).astype(bf16)
        qk_ref[d, rs, cs] = (qk[i] * decay[j]).astype(bf16)
    for s in range(DN_PRE_CHUNKS):
        for d in range(N_DIR):
            last = s * n + (n - 1 if d == 0 else 0)
            eg_ref[d, s * DN_EG_ROWS:(s + 1) * DN_EG_ROWS, :] = jnp.broadcast_to(
                jnp.exp(g_ref[d, last:last + 1, :]), (DN_EG_ROWS, LANES))


def _delta_pre(q, k, v, gates):
    b, l, c = q.shape
    rows = DN_PRE_CHUNKS * DN_CHUNK
    n = l // rows
    row = pl.BlockSpec((None, rows, c), lambda bi, i: (bi, i, 0))
    out = pl.BlockSpec((N_DIR, None, rows, c), lambda bi, i: (0, bi, i, 0))
    eg_rows = DN_PRE_CHUNKS * DN_EG_ROWS
    return pl.pallas_call(
        _delta_pre_kernel,
        grid=(b, n),
        in_specs=[row, row, row, pl.BlockSpec((N_DIR, None, rows, LANES), lambda bi, i: (0, bi, i, 0))],
        out_specs=[out] * 5 + [pl.BlockSpec((N_DIR, None, eg_rows, LANES), lambda bi, i: (0, bi, i, 0))],
        out_shape=[jax.ShapeDtypeStruct((N_DIR, b, l, c), f32)]
        + [jax.ShapeDtypeStruct((N_DIR, b, l, c), bf16)] * 4
        + [jax.ShapeDtypeStruct((N_DIR, b, n * eg_rows, LANES), f32)],
        compiler_params=_cparams(("arbitrary", "arbitrary")),
        name="delta_pre",
    )(q, k, v, gates)


def _delta_scan_kernel(*refs):
    n_in = 6 * N_DIR
    o_refs = refs[n_in:n_in + N_DIR]
    s_ref = refs[n_in + N_DIR]
    n = DN_CHUNK

    @pl.when(pl.program_id(1) == 0)
    def _():
        s_ref[...] = jnp.zeros(s_ref.shape, f32)

    dh = [(d, h) for d in range(N_DIR) for h in range(DN_HEADS)]
    csl = lambda h: slice(h * LANES, (h + 1) * LANES)
    ins = lambda d: refs[6 * d:6 * d + 6]
    s = [s_ref[d, h] for d, h in dh]
    r = [_dot(jnp.concatenate([ins(d)[1][:, csl(h)], ins(d)[2][:, csl(h)]], axis=0), si)
         for (d, h), si in zip(dh, s)]
    v_new = [ins(d)[0][:, csl(h)] - ri[0:n] for (d, h), ri in zip(dh, r)]
    s_add = [_dot_tn(ins(d)[3][:, csl(h)], vi) for (d, h), vi in zip(dh, v_new)]
    o_add = [_dot(ins(d)[4][:, h * LANES:h * LANES + n], vi) for (d, h), vi in zip(dh, v_new)]
    for j, (d, h) in enumerate(dh):
        s_ref[d, h] = s[j] * ins(d)[5][0:1, h:h + 1] + s_add[j]
        o_refs[d][:, csl(h)] = r[j][n:2 * n] + o_add[j]


def _delta_scan(u, w, qe, kd, qk, eg):
    _, b, l, c = u.shape
    nc = l // DN_CHUNK
    ctx_chunks = TM // DN_CHUNK

    def chunk(d, ci):
        if d == 0:
            return ci
        return jnp.where(ci < ctx_chunks, ctx_chunks - 1 - ci, nc + ctx_chunks - 1 - ci)

    in_specs, args = [], []
    for d in range(N_DIR):
        spec = pl.BlockSpec((None, None, DN_CHUNK, c), lambda bi, ci, d=d: (d, bi, chunk(d, ci), 0))
        in_specs += [spec] * 5 + [pl.BlockSpec((None, None, DN_EG_ROWS, LANES),
                                               lambda bi, ci, d=d: (d, bi, chunk(d, ci), 0))]
        args += [u, w, qe, kd, qk, eg]
    return pl.pallas_call(
        _delta_scan_kernel,
        grid=(b, nc),
        in_specs=in_specs,
        out_specs=[pl.BlockSpec((None, DN_CHUNK, c), lambda bi, ci, d=d: (bi, chunk(d, ci), 0))
                   for d in range(N_DIR)],
        out_shape=[jax.ShapeDtypeStruct((b, l, c), f32)] * N_DIR,
        scratch_shapes=[pltpu.VMEM((N_DIR, DN_HEADS, DN_DK, LANES), f32)],
        compiler_params=_cparams(("arbitrary", "arbitrary")),
        name="delta_scan",
    )(*args)


def _rope_tables(s, ctx_len):
    rows = s // GRID_W
    row = jnp.repeat(jnp.arange(rows, dtype=i32), GRID_W)
    col = jnp.tile(jnp.arange(GRID_W, dtype=i32), rows)
    freqs = ROPE_THETA ** (-jnp.arange(ROPE_PAIRS, dtype=f32) / ROPE_PAIRS)
    ang = jnp.stack([row, col], axis=-1).astype(f32)[..., None] * freqs
    cos = jnp.cos(ang)
    sin = jnp.sin(ang)
    cos_h = jnp.concatenate([cos[:, 0], cos[:, 0], cos[:, 1], cos[:, 1]], axis=-1)
    sin_h = jnp.concatenate([-sin[:, 0], sin[:, 0], -sin[:, 1], sin[:, 1]], axis=-1)
    cos_t = jnp.concatenate([jnp.ones((ctx_len, HEAD_DIM), f32), cos_h], axis=0)
    sin_t = jnp.concatenate([jnp.zeros((ctx_len, HEAD_DIM), f32), sin_h], axis=0)
    return jnp.tile(cos_t, (1, 2)), jnp.tile(sin_t, (1, 2))


def _pad_lanes(v, n=LANES):
    v = v.reshape(1, -1)
    return jnp.pad(v, ((0, 0), (0, n - v.shape[1])))


def kernel(x, c, ctx, c_ctx, mod_w, mod_b, norm1_g, norm2_g, ab_w_in, ab_q_norm, ab_k_norm, gm_ln_g, gm_ln_b, gm_w_s, gm_b_s, ab_w_out, cd_w_in, cv_dw_w, cv_dw_b, cv_ln_g, cv_ln_b, dn_conv_w, dn_a_log, dn_dt_bias, dn_o_norm, cd_w_out, router_w, router_b, moe_w_gate, moe_w_up, moe_w_down, final_norm_g):
    b, s, d = x.shape
    ctx_len = ctx.shape[1]
    assert ctx_len == TM and s % (2 * TM) == 0 and mod_w.shape[0] == 2
    l = ctx_len + s
    nt = l // TM

    n_rows = -(-(b + 1) // 8) * 8
    cc = jnp.concatenate([c, c_ctx[None, :], jnp.zeros((n_rows - b - 1, d), f32)], axis=0)
    mod = _modulation(cc, mod_w, mod_b)
    mod_lat = mod[:, :b].reshape(2, b, 1, 6, d)
    mod_ctx = jnp.broadcast_to(mod[:, b].reshape(2, 1, 1, 6, d), (2, b, 1, 6, d))
    modl = jnp.concatenate([mod_ctx, mod_lat], axis=2)

    xs = jnp.concatenate([ctx, x], axis=1)
    rw = jnp.pad(router_w, ((0, 0), (0, LANES - N_EXPERTS)))
    rb = router_b.reshape(N_EXPERTS, 1)
    wg = moe_w_gate.astype(bf16)
    wu = moe_w_up.astype(bf16)
    wd = moe_w_down.astype(bf16)

    cos_t, sin_t = _rope_tables(s, ctx_len)
    lane = jnp.arange(2 * LANES)
    seg = ((lane[:, None] // HEAD_DIM) == (lane[None, :] // HEAD_DIM)).astype(bf16) * (1.0 / HEAD_DIM)
    q, kd, vt, sg = _ab_in_proj(
        xs, modl[0], norm1_g[0:1], ab_w_in[0].astype(bf16), seg,
        jnp.tile(ab_q_norm[0], 2)[None, :], jnp.tile(ab_k_norm[0], 2)[None, :], cos_t, sin_t,
        gm_ln_g[0:1], gm_ln_b[0:1], gm_w_s[0].astype(bf16),
        jnp.broadcast_to(gm_b_s[0][:, :, None], (GM_GROUPS, GM_CHUNK, LANES)))
    att = _attention(q, kd, vt)
    x1, h1, idx, wts = _out_proj(att, sg, xs, modl[0], norm2_g[0:1], ab_w_out[0].astype(bf16), rw, rb,
                                 delta=False, lat_only=False)
    xs = _moe(h1, x1, idx, wts, modl[0], wg[0], wu[0], wd[0], nt, False).reshape(b, l, d)

    c1 = 2 * CV_CH
    c2 = c1 + DN_QKV
    c3 = c2 + 2 * N_DIR * DN_HEADS
    w_in = cd_w_in[0]
    w_cd = jnp.concatenate([w_in[:, 0:c2], w_in[:, c3:], w_in[:, c2:c3],
                            jnp.zeros((d, LANES - (c3 - c2)), f32)], axis=1).astype(bf16)
    ag, qkv, dg, ab = _cd_in_proj(xs, modl[1], norm1_g[1:2], w_cd)
    conv = _conformer(ag, cv_dw_w[0], cv_dw_b[0:1], cv_ln_g[0:1], cv_ln_b[0:1])
    qn, kn, vv, gates = _delta_features(qkv, ab, dn_conv_w[0], _pad_lanes(dn_a_log[0]), _pad_lanes(dn_dt_bias[0]))
    o_f, o_b = _delta_scan(*_delta_pre(qn, kn, vv, gates))
    x2, h2, idx, wts = _out_proj(conv, (o_f, o_b, dg, dn_o_norm[0:1]), xs, modl[1], norm2_g[1:2],
                                 cd_w_out[0].astype(bf16), rw, rb, delta=True, lat_only=True)
    out = _moe(h2, x2, idx, wts, modl[1], wg[1], wu[1], wd[1], nt - 1, True, final_norm_g[None, :])
    return out.reshape(b, s, d)
```

```python
import functools
import math

import jax
import jax.numpy as jnp
from jax import lax
from jax.experimental import pallas as pl
from jax.experimental.pallas import tpu as pltpu

f32 = jnp.float32
bf16 = jnp.bfloat16
i32 = jnp.int32

GRID_W = 64
EPS = 1e-6
ATT_HEADS = 8
ATT_KV_HEADS = 2
HEAD_DIM = 64
ROPE_PAIRS = HEAD_DIM // 4
ROPE_THETA = 10000.0
A_Q = ATT_HEADS * HEAD_DIM
A_KV = ATT_KV_HEADS * HEAD_DIM
GM_GROUPS = 4
GM_CHUNK = 128
GM_CH = 512
CV_CH = 512
CV_WIDTH = 31
DN_HEADS = 4
DN_DK = 128
DN_CONV = 5
DN_CHUNK = 64
N_DIR = 2
DN_QK = DN_HEADS * DN_DK
DN_QKV = 3 * DN_QK
N_EXPERTS = 16
N_GROUPS = 4
EXPERT_FF = 512

LANES = 128
SUBLANES = 8
TM = 256
V_ROWS = HEAD_DIM + 16
VMEM_LIMIT = 48 * 1024 * 1024


def _cparams(sem):
    return pltpu.CompilerParams(dimension_semantics=sem, vmem_limit_bytes=VMEM_LIMIT)


def _sigmoid(x):
    return 1.0 / (1.0 + jnp.exp(-x))


def _silu(x):
    return x * _sigmoid(x)


def _gelu(x):
    return x * (0.5 * (1.0 + jnp.tanh(math.sqrt(2.0 / math.pi) * (x + 0.044715 * (x * x * x)))))


def _rms_rows(x):
    return x * lax.rsqrt(jnp.mean(x * x, axis=-1, keepdims=True) + EPS)


def _layer_norm_rows(x, g, b):
    mu = jnp.mean(x, axis=-1, keepdims=True)
    xc = x - mu
    var = jnp.mean(xc * xc, axis=-1, keepdims=True)
    return xc * lax.rsqrt(var + EPS) * g + b


def _dot(a, b):
    return jnp.dot(a.astype(bf16), b.astype(bf16), preferred_element_type=f32)


def _dot_nt(a, b):
    return lax.dot_general(a.astype(bf16), b.astype(bf16), (((1,), (1,)), ((), ())), preferred_element_type=f32)


def _dot_tn(a, b):
    return lax.dot_general(a.astype(bf16), b.astype(bf16), (((0,), (0,)), ((), ())), preferred_element_type=f32)


def _mod_kernel(c_ref, w_ref, b_ref, o_ref):
    c = c_ref[...]
    o_ref[...] = jnp.dot(_silu(c), w_ref[...], precision=lax.Precision.HIGHEST,
                         preferred_element_type=f32) + b_ref[...]


def _modulation(cc, mod_w, mod_b):
    depth, d, n = mod_w.shape
    tn = 1024
    rows = cc.shape[0]
    return pl.pallas_call(
        _mod_kernel,
        grid=(depth, n // tn),
        in_specs=[pl.BlockSpec((rows, d), lambda l, j: (0, 0)),
                  pl.BlockSpec((None, d, tn), lambda l, j: (l, 0, j)),
                  pl.BlockSpec((None, 1, tn), lambda l, j: (l, 0, j))],
        out_specs=pl.BlockSpec((None, rows, tn), lambda l, j: (l, 0, j)),
        out_shape=jax.ShapeDtypeStruct((depth, rows, n), f32),
        compiler_params=_cparams(("arbitrary", "arbitrary")),
        name="modulation",
    )(cc, mod_w, mod_b.reshape(depth, 1, n))


def _mod_index(b, i):
    return (b, jnp.minimum(i, 1), 0, 0)


def _head_norm_rope(x, seg_mean, gain, cos, sin, scale):
    xx = x * x
    outs = []
    lane = lax.broadcasted_iota(i32, (x.shape[0], LANES), 1)
    first = (lane % 32) < 16
    for c0 in range(0, x.shape[1], 2 * LANES):
        w = min(2 * LANES, x.shape[1] - c0)
        ms = _dot(xx[:, c0:c0 + w], seg_mean[0:w, 0:w])
        xn = x[:, c0:c0 + w] * lax.rsqrt(ms + EPS)
        for c1 in range(0, w, LANES):
            y = xn[:, c1:c1 + LANES] * gain
            partner = jnp.where(first, pltpu.roll(y, LANES - 16, 1), pltpu.roll(y, 16, 1))
            outs.append((y * cos + partner * sin) * scale)
    return outs


def _tile_rows(ctx_ref, x_ref):
    return jnp.where(pl.program_id(1) == 0, ctx_ref[...], x_ref[...])


def _ctx_lat_specs(d):
    return [pl.BlockSpec((None, TM, d), lambda bi, i: (bi, 0, 0)),
            pl.BlockSpec((None, TM, d), lambda bi, i: (bi, jnp.maximum(i - 1, 0), 0))]


def _ab_in_kernel(ctx_ref, x_ref, mod_ref, g_ref, w_ref, seg_ref, qg_ref, kg_ref, cos_ref, sin_ref,
                  lng_ref, lnb_ref, ws_ref, bs_ref, q_ref, kd_ref, vt_ref, s_ref):
    mod = mod_ref[...]
    h = _rms_rows(_tile_rows(ctx_ref, x_ref)) * g_ref[...] * (1.0 + mod[1:2]) + mod[0:1]
    acc = _dot(h, w_ref[...])
    cos = cos_ref[...]
    sin = sin_ref[...]
    seg = seg_ref[...]

    q = _head_norm_rope(acc[:, 0:A_Q], seg, qg_ref[...], cos, sin, HEAD_DIM ** -0.5 * math.log2(math.e))
    for c, qc in enumerate(q):
        q_ref[:, c * LANES:(c + 1) * LANES] = qc.astype(bf16)

    (k,) = _head_norm_rope(acc[:, A_Q:A_Q + A_KV], seg, kg_ref[...], cos, sin, 1.0)
    lane = lax.broadcasted_iota(i32, k.shape, 1)
    swapped = pltpu.roll(k, HEAD_DIM, 1)
    kd_ref[0] = jnp.where(lane < HEAD_DIM, k, swapped).astype(bf16)
    kd_ref[1] = jnp.where(lane < HEAD_DIM, swapped, k).astype(bf16)

    vt = acc[:, A_Q + A_KV:A_Q + 2 * A_KV].T
    ones = jnp.ones((V_ROWS - HEAD_DIM, vt.shape[1]), bf16)
    for j in range(ATT_KV_HEADS):
        vt_ref[j, 0:HEAD_DIM, :] = vt[j * HEAD_DIM:(j + 1) * HEAD_DIM].astype(bf16)
        vt_ref[j, HEAD_DIM:V_ROWS, :] = ones

    c0 = A_Q + 2 * A_KV
    gu = _gelu(acc[:, c0:c0 + GM_CH])
    ln = _layer_norm_rows(_gelu(acc[:, c0 + GM_CH:c0 + 2 * GM_CH]), lng_ref[...], lnb_ref[...])
    for n in range(acc.shape[0] // GM_CHUNK):
        r = slice(n * GM_CHUNK, (n + 1) * GM_CHUNK)
        for g in range(GM_GROUPS):
            cs = slice(g * LANES, (g + 1) * LANES)
            mixed = _dot(ws_ref[g], ln[r, cs]) + bs_ref[g]
            s_ref[r, cs] = (gu[r, cs] * mixed).astype(bf16)


def _ab_in_proj(ctx, x, modl, g, w, seg, qg, kg, cos, sin, lng, lnb, ws, bs):
    b, s, d = x.shape
    l = s + ctx.shape[1]
    nt = l // TM
    n = w.shape[1]
    full = lambda shape: pl.BlockSpec(shape, lambda bi, i: (0,) * len(shape))
    return pl.pallas_call(
        _ab_in_kernel,
        grid=(b, nt),
        in_specs=_ctx_lat_specs(d) + [
                  pl.BlockSpec((None, None, 6, d), _mod_index),
                  full((1, d)), full((d, n)), full((2 * LANES, 2 * LANES)),
                  full((1, LANES)), full((1, LANES)),
                  pl.BlockSpec((TM, LANES), lambda bi, i: (i, 0)),
                  pl.BlockSpec((TM, LANES), lambda bi, i: (i, 0)),
                  full((1, GM_CH)), full((1, GM_CH)),
                  full((GM_GROUPS, GM_CHUNK, GM_CHUNK)), full((GM_GROUPS, GM_CHUNK, LANES))],
        out_specs=[pl.BlockSpec((None, TM, A_Q), lambda bi, i: (bi, i, 0)),
                   pl.BlockSpec((None, ATT_KV_HEADS, TM, LANES), lambda bi, i: (bi, 0, i, 0)),
                   pl.BlockSpec((None, ATT_KV_HEADS, None, V_ROWS, TM), lambda bi, i: (bi, 0, i, 0, 0)),
                   pl.BlockSpec((None, TM, GM_CH), lambda bi, i: (bi, i, 0))],
        out_shape=[jax.ShapeDtypeStruct((b, l, A_Q), bf16),
                   jax.ShapeDtypeStruct((b, ATT_KV_HEADS, l, LANES), bf16),
                   jax.ShapeDtypeStruct((b, ATT_KV_HEADS, nt, V_ROWS, TM), bf16),
                   jax.ShapeDtypeStruct((b, l, GM_CH), bf16)],
        compiler_params=_cparams(("arbitrary", "arbitrary")),
        name="ab_in_proj",
    )(ctx, x, modl, g, w, seg, qg, kg, cos, sin, lng, lnb, ws, bs)


def _attn_kernel(q_ref, k_ref, vt_ref, o_ref, acc_ref, m_ref, st_ref, st2_ref, *, n_pairs):
    i = pl.program_id(2)
    q = q_ref[...]
    lane = lax.broadcasted_iota(i32, (TM, LANES), 1)
    zero = jnp.zeros((TM, LANES), bf16)
    qs = []
    for p in range(2):
        qp = q[:, p * LANES:(p + 1) * LANES]
        qs.append(jnp.where(lane < HEAD_DIM, qp, zero))
        qs.append(jnp.where(lane < HEAD_DIM, zero, qp))
    heads = range(len(qs))
    hsl = lambda h: slice(h * TM, (h + 1) * TM)
    m_ref[...] = jnp.full(m_ref.shape, -jnp.inf, f32)
    acc_ref[...] = jnp.zeros(acc_ref.shape, f32)

    def scores(kb, h):
        return _dot_nt(kb, qs[h])

    def update(st, vb, h):
        m_old = m_ref[:, hsl(h)]
        m_new = jnp.maximum(m_old, jnp.max(st, axis=0, keepdims=True))
        p = jnp.exp2(st - m_new).astype(bf16)
        acc_ref[:, hsl(h)] = (acc_ref[:, hsl(h)] * jnp.exp2(m_old - m_new)
                              + jnp.dot(vb, p, preferred_element_type=f32))
        m_ref[:, hsl(h)] = m_new

    def latent_keys(t):
        return k_ref[pl.ds(pl.multiple_of(TM + t * (2 * TM), TM), 2 * TM), :]

    def latent_values(t):
        return jnp.concatenate([vt_ref[1 + 2 * t], vt_ref[2 + 2 * t]], axis=1)

    kb0 = k_ref[0:TM, :]

    @pl.when(i == 0)
    def _():
        for h in heads:
            update(scores(kb0, h), vt_ref[0], h)

    @pl.when(i > 0)
    def _():
        def step(t, cur, nxt):
            kb_next = latent_keys(t + 1)
            vb = latent_values(t)
            for h in heads:
                nxt[:, hsl(h)] = scores(kb_next, h)
                update(cur[:, hsl(h)], vb, h)

        kb = latent_keys(0)
        st_c = [scores(kb0, h) for h in heads]
        for h in heads:
            st_ref[:, hsl(h)] = scores(kb, h)
            update(st_c[h], vt_ref[0], h)

        def body(j, carry):
            step(2 * j, st_ref, st2_ref)
            step(2 * j + 1, st2_ref, st_ref)
            return carry
        lax.fori_loop(0, (n_pairs - 2) // 2, body, 0)
        step(n_pairs - 2, st_ref, st2_ref)
        vb = latent_values(n_pairs - 1)
        for h in heads:
            update(st2_ref[:, hsl(h)], vb, h)

    acc = acc_ref[...]
    o = acc[0:HEAD_DIM] / acc[HEAD_DIM:HEAD_DIM + 1]
    for p in range(2):
        pair = jnp.concatenate([o[:, (2 * p) * TM:(2 * p + 1) * TM],
                                o[:, (2 * p + 1) * TM:(2 * p + 2) * TM]], axis=0)
        o_ref[:, p * LANES:(p + 1) * LANES] = pair.T.astype(bf16)


def _attention(q, kd, vt):
    b, l, _ = q.shape
    nt = l // TM
    group_w = A_Q // ATT_KV_HEADS
    assert (l - TM) % (4 * TM) == 0
    return pl.pallas_call(
        functools.partial(_attn_kernel, n_pairs=(l - TM) // (2 * TM)),
        grid=(b, ATT_KV_HEADS, nt),
        in_specs=[pl.BlockSpec((None, TM, group_w), lambda bi, j, i: (bi, i, j)),
                  pl.BlockSpec((None, None, l, LANES), lambda bi, j, i: (bi, j, 0, 0)),
                  pl.BlockSpec((None, None, nt, V_ROWS, TM), lambda bi, j, i: (bi, j, 0, 0, 0))],
        out_specs=pl.BlockSpec((None, TM, group_w), lambda bi, j, i: (bi, i, j)),
        out_shape=jax.ShapeDtypeStruct((b, l, A_Q), bf16),
        scratch_shapes=[pltpu.VMEM((V_ROWS, 4 * TM), f32), pltpu.VMEM((1, 4 * TM), f32),
                        pltpu.VMEM((2 * TM, 4 * TM), f32), pltpu.VMEM((2 * TM, 4 * TM), f32)],
        compiler_params=_cparams(("arbitrary", "arbitrary", "arbitrary")),
        name="attention",
    )(q, kd, vt)


def _route_rows(scores, sel):
    s = [sel[e:e + 1] for e in range(N_EXPERTS)]
    sc = [scores[e:e + 1] for e in range(N_EXPERTS)]
    per = N_EXPERTS // N_GROUPS
    gs = []
    for g in range(N_GROUPS):
        a, b, c, d = s[per * g:per * g + per]
        hi1, lo1 = jnp.maximum(a, b), jnp.minimum(a, b)
        hi2, lo2 = jnp.maximum(c, d), jnp.minimum(c, d)
        gs.append(jnp.maximum(hi1, hi2) + jnp.maximum(jnp.minimum(hi1, hi2), jnp.maximum(lo1, lo2)))
    best = jnp.zeros(gs[0].shape, i32)
    best_v = gs[0]
    for g in range(1, N_GROUPS):
        better = gs[g] > best_v
        best = jnp.where(better, g, best)
        best_v = jnp.where(better, gs[g], best_v)
    v, w = [], []
    for j in range(per):
        vj, wj = s[j], sc[j]
        for g in range(1, N_GROUPS):
            vj = jnp.where(best == g, s[per * g + j], vj)
            wj = jnp.where(best == g, sc[per * g + j], wj)
        v.append(vj)
        w.append(wj)
    i1 = jnp.zeros(best.shape, i32)
    m1, w1 = v[0], w[0]
    for j in range(1, per):
        better = v[j] > m1
        i1 = jnp.where(better, j, i1)
        m1 = jnp.where(better, v[j], m1)
        w1 = jnp.where(better, w[j], w1)
    i2 = jnp.zeros(best.shape, i32)
    m2 = jnp.full(m1.shape, -jnp.inf, f32)
    w2 = jnp.zeros(m1.shape, f32)
    for j in range(per):
        cand = jnp.where(i1 == j, -jnp.inf, v[j])
        better = cand > m2
        i2 = jnp.where(better, j, i2)
        m2 = jnp.where(better, cand, m2)
        w2 = jnp.where(better, w[j], w2)
    tot = w1 + w2
    return best * per + i1, best * per + i2, w1 / tot, w2 / tot


def _out_kernel(*refs, delta):
    if delta:
        (a1_ref, of_ref, ob_ref, dg_ref, on_ref, x_ref, mod_ref, g_ref, w_ref, rw_ref, rb_ref,
         xo_ref, h_ref, idx_ref, wt_ref) = refs
        o = of_ref[...] + ob_ref[...]
        dg = dg_ref[...]
        parts = []
        for h in range(DN_HEADS):
            cs = slice(h * LANES, (h + 1) * LANES)
            parts.append(_rms_rows(o[:, cs]) * on_ref[...] * _silu(dg[:, cs]))
        a2 = jnp.concatenate(parts, axis=1)
        x = x_ref[...]
    else:
        (a1_ref, a2_ref, ctx_ref, x_ref, mod_ref, g_ref, w_ref, rw_ref, rb_ref,
         xo_ref, h_ref, idx_ref, wt_ref) = refs
        a2 = a2_ref[...]
        x = _tile_rows(ctx_ref, x_ref)
    half = a1_ref.shape[-1]
    y = _dot(a1_ref[...], w_ref[0:half, :]) + _dot(a2, w_ref[half:2 * half, :])
    mod = mod_ref[...]
    xn = x + mod[2:3] * y
    xo_ref[...] = xn
    h2 = _rms_rows(xn) * g_ref[...] * (1.0 + mod[4:5]) + mod[3:4]
    h_ref[...] = h2
    h_hi = h2.astype(bf16)
    h_lo = (h2 - h_hi.astype(f32)).astype(bf16)
    part = jnp.dot(h_hi, rw_ref[...], preferred_element_type=f32)
    logits = (part[:, 0:LANES] + part[:, LANES:2 * LANES]
              + jnp.dot(h_lo, rw_ref[:, 0:LANES], preferred_element_type=f32))
    scores = _sigmoid(logits.T[0:N_EXPERTS])
    e1, e2, w1, w2 = _route_rows(scores, scores + rb_ref[...])
    idx_ref[0:1, :] = e1
    idx_ref[1:2, :] = e2
    wt_ref[0:1, :] = w1
    wt_ref[1:2, :] = w2


def _out_proj(a1, a2s, x, modl, g, w, rw, rb, *, delta):
    if delta:
        b, l, d = x.shape
        x_specs, x_args, off = [pl.BlockSpec((None, TM, d), lambda bi, i: (bi, i + 1, 0))], [x], 1
    else:
        b, s, d = x[1].shape
        l = s + x[0].shape[1]
        x_specs, x_args, off = _ctx_lat_specs(d), list(x), 0
    nt = l // TM
    nto = nt - off
    half = a1.shape[-1]
    full = lambda shape: pl.BlockSpec(shape, lambda bi, i: (0,) * len(shape))
    row_l = lambda w_: pl.BlockSpec((None, TM, w_), lambda bi, i: (bi, i + off, 0))
    if delta:
        a2_specs = [row_l(half), row_l(half), row_l(half), full((1, LANES))]
        a2_args = list(a2s)
        a1_spec = pl.BlockSpec((None, TM, half), lambda bi, i: (bi, i, 0))
    else:
        a2_specs = [row_l(half)]
        a2_args = [a2s]
        a1_spec = row_l(half)
    t_out = b * nto * TM
    flat = lambda bi, i: (bi * nto + i, 0)
    return pl.pallas_call(
        functools.partial(_out_kernel, delta=delta),
        grid=(b, nto),
        in_specs=[a1_spec] + a2_specs + x_specs + [
            pl.BlockSpec((None, None, 6, d), lambda bi, i: (bi, jnp.minimum(i + off, 1), 0, 0)),
            full((1, d)), full((2 * half, d)), full((d, 2 * LANES)), full((N_EXPERTS, 1))],
        out_specs=[pl.BlockSpec((TM, d), flat), pl.BlockSpec((TM, d), flat),
                   pl.BlockSpec((2, TM), lambda bi, i: (0, bi * nto + i)),
                   pl.BlockSpec((2, TM), lambda bi, i: (0, bi * nto + i))],
        out_shape=[jax.ShapeDtypeStruct((t_out, d), f32), jax.ShapeDtypeStruct((t_out, d), f32),
                   jax.ShapeDtypeStruct((2, t_out), i32), jax.ShapeDtypeStruct((2, t_out), f32)],
        compiler_params=_cparams(("arbitrary", "arbitrary")),
        name="out_proj_delta" if delta else "out_proj",
    )(a1, *a2_args, *x_args, modl, g, w, rw, rb)


MOE_GROUP = 8
MOE_BUF = 2 * TM + N_EXPERTS * MOE_GROUP
MOE_GROUPS = MOE_BUF // MOE_GROUP


def _dispatch_plan(idx):
    t = idx.shape[1]
    nt = t // TM
    e = idx.reshape(2, nt, TM).transpose(1, 0, 2).reshape(nt, 2 * TM)
    onehot = (e[:, :, None] == jnp.arange(N_EXPERTS, dtype=i32)).astype(i32)
    lrank = jnp.cumsum(onehot, axis=1) - onehot
    cnt = jnp.sum(onehot, axis=1)
    cnt_g = (cnt + MOE_GROUP - 1) // MOE_GROUP * MOE_GROUP
    boff = jnp.cumsum(cnt_g, axis=1) - cnt_g
    bufpos = jnp.sum((boff[:, None, :] + lrank) * onehot, axis=2)
    region = jnp.sum(cnt_g, axis=0)
    padded = (region + TM - 1) // TM * TM
    ends = jnp.cumsum(padded)
    base = ends - padded
    gstart = base[None, :] + jnp.cumsum(cnt_g, axis=0) - cnt_g
    n_tiles = -(-(2 * t + nt * N_EXPERTS * (MOE_GROUP - 1)) // TM) + N_EXPERTS
    n_used = (ends[-1] // TM).astype(i32)
    tiles = jnp.arange(n_tiles, dtype=i32)
    tile_e = jnp.sum((tiles[:, None] * TM >= ends[None, :]).astype(i32), axis=1)
    last_e = jnp.sum(((n_used - 1) * TM >= ends).astype(i32))
    tile_e = jnp.where(tiles < n_used, tile_e, last_e)
    g_row = jnp.arange(MOE_GROUPS, dtype=i32) * MOE_GROUP
    g_exp = jnp.sum((g_row[None, :, None] >= (boff + cnt_g)[:, None, :]).astype(i32), axis=2)
    g_sel = (g_exp[:, :, None] == jnp.arange(N_EXPERTS, dtype=i32)).astype(i32)
    g_dst = jnp.sum(g_sel * (gstart - boff)[:, None, :], axis=2) + g_row[None, :]
    runs = (g_dst.reshape(-1), jnp.sum(cnt_g, axis=1) // MOE_GROUP)
    gaps = (jnp.concatenate([base + region, ends[-1:]]),
            jnp.concatenate([(padded - region) // MOE_GROUP, n_tiles - n_used.reshape(1)]))
    return runs, gaps, bufpos.reshape(nt, 2, TM), tile_e, n_used.reshape(1), n_tiles


def _run_copies(runs, tile, buf, hbm, sem, *, to_hbm, start):
    row_ref, ng_ref = runs

    def body(c, carry):
        v = buf.at[pl.ds(pl.multiple_of(c * MOE_GROUP, MOE_GROUP) if start else 0, MOE_GROUP)]
        row = pl.multiple_of(row_ref[tile * MOE_GROUPS + c], MOE_GROUP) if start else 0
        h = hbm.at[pl.ds(row, MOE_GROUP)]
        cp = pltpu.make_async_copy(v, h, sem) if to_hbm else pltpu.make_async_copy(h, v, sem)
        if start:
            cp.start()
        else:
            cp.wait()
        return carry
    lax.fori_loop(0, ng_ref[tile], body, 0)


def _moe_dispatch_kernel(row_ref, ng_ref, gap0_ref, gapn_ref, h_ref, bp_ref, xs_hbm, buf, zbuf, sem, zsem):
    runs = (row_ref, ng_ref)
    i = pl.program_id(0)
    n = pl.num_programs(0)
    slot = i % 2
    bp = bp_ref[...]
    j = lax.broadcasted_iota(i32, (MOE_BUF, TM), 0)
    sel = jnp.where((j == bp[0:1]) | (j == bp[1:2]), 1.0, 0.0).astype(bf16)
    buf[slot] = jnp.dot(sel, h_ref[...].astype(bf16), preferred_element_type=f32)
    _run_copies(runs, i, buf.at[slot], xs_hbm, sem.at[slot], to_hbm=True, start=True)

    @pl.when(i > 0)
    def _():
        _run_copies(runs, i - 1, buf.at[1 - slot], xs_hbm, sem.at[1 - slot], to_hbm=True, start=False)

    @pl.when(i == n - 1)
    def _():
        _run_copies(runs, i, buf.at[slot], xs_hbm, sem.at[slot], to_hbm=True, start=False)
        zbuf[...] = jnp.zeros(zbuf.shape, f32)
        for start in (True, False):
            for e in range(N_EXPERTS + 1):
                rows = MOE_GROUP if e < N_EXPERTS else TM

                def body(c, carry, e=e, start=start, rows=rows):
                    dst = xs_hbm.at[pl.ds(pl.multiple_of(gap0_ref[e] + c * rows, MOE_GROUP), rows)]
                    cp = pltpu.make_async_copy(zbuf.at[pl.ds(0, rows)], dst, zsem)
                    if start:
                        cp.start()
                    else:
                        cp.wait()
                    return carry
                lax.fori_loop(0, gapn_ref[e], body, 0)


def _moe_dispatch(h, runs, gaps, bufpos, n_tiles):
    t, d = h.shape
    grid_spec = pltpu.PrefetchScalarGridSpec(
        num_scalar_prefetch=4,
        grid=(t // TM,),
        in_specs=[pl.BlockSpec((TM, d), lambda i, *_: (i, 0)),
                  pl.BlockSpec((None, 2, TM), lambda i, *_: (i, 0, 0))],
        out_specs=pl.BlockSpec(memory_space=pl.ANY),
        scratch_shapes=[pltpu.VMEM((2, MOE_BUF, d), f32), pltpu.VMEM((TM, d), f32),
                        pltpu.SemaphoreType.DMA((2,)), pltpu.SemaphoreType.DMA],
    )
    return pl.pallas_call(
        _moe_dispatch_kernel,
        grid_spec=grid_spec,
        out_shape=jax.ShapeDtypeStruct((n_tiles * TM, d), f32),
        compiler_params=_cparams(("arbitrary",)),
        name="moe_dispatch",
    )(*runs, *gaps, h, bufpos)


def _moe_ffn_kernel(te_ref, nu_ref, x_ref, wg_ref, wu_ref, wd_ref, y_ref):
    @pl.when(pl.program_id(0) < nu_ref[0])
    def _():
        x = x_ref[...].astype(bf16)
        y = None
        for c in range(0, wg_ref.shape[1], TM):
            a = _silu(_dot(x, wg_ref[:, c:c + TM])) * _dot(x, wu_ref[:, c:c + TM])
            part = _dot(a, wd_ref[c:c + TM, :])
            y = part if y is None else y + part
        y_ref[...] = y

    @pl.when(pl.program_id(0) >= nu_ref[0])
    def _():
        y_ref[...] = jnp.zeros(y_ref.shape, f32)


def _moe_ffn(xs, tile_e, n_used, wg, wu, wd, layer):
    p, d = xs.shape
    ff = wg.shape[3]
    row = lambda i, te, nu: (jnp.minimum(i, nu[0] - 1), 0)
    grid_spec = pltpu.PrefetchScalarGridSpec(
        num_scalar_prefetch=2,
        grid=(p // TM,),
        in_specs=[pl.BlockSpec((TM, d), row),
                  pl.BlockSpec((None, None, d, ff), lambda i, te, nu: (layer, te[i], 0, 0)),
                  pl.BlockSpec((None, None, d, ff), lambda i, te, nu: (layer, te[i], 0, 0)),
                  pl.BlockSpec((None, None, ff, d), lambda i, te, nu: (layer, te[i], 0, 0))],
        out_specs=pl.BlockSpec((TM, d), lambda i, te, nu: (i, 0)),
    )
    return pl.pallas_call(
        _moe_ffn_kernel,
        grid_spec=grid_spec,
        out_shape=jax.ShapeDtypeStruct((p, d), f32),
        compiler_params=_cparams(("arbitrary",)),
        name="moe_ffn",
    )(tile_e, n_used, xs, wg, wu, wd)


def _moe_combine_kernel(row_ref, ng_ref, y_hbm, x_ref, bp_ref, wt_ref, mod_ref, *rest, final):
    if final:
        fg_ref, o_ref, ybuf, sem = rest
    else:
        o_ref, ybuf, sem = rest
    runs = (row_ref, ng_ref)
    i = pl.program_id(0)
    n = pl.num_programs(0)
    slot = i % 2

    @pl.when(i == 0)
    def _():
        ybuf[...] = jnp.zeros(ybuf.shape, f32)
        _run_copies(runs, 0, ybuf.at[0], y_hbm, sem.at[0], to_hbm=False, start=True)

    @pl.when(i + 1 < n)
    def _():
        _run_copies(runs, i + 1, ybuf.at[1 - slot], y_hbm, sem.at[1 - slot], to_hbm=False, start=True)

    _run_copies(runs, i, ybuf.at[slot], y_hbm, sem.at[slot], to_hbm=False, start=False)
    bp = bp_ref[...]
    w = wt_ref[...]
    j = lax.broadcasted_iota(i32, (MOE_BUF, TM), 0)
    sel_t = jnp.where(j == bp[0:1], w[0:1], 0.0) + jnp.where(j == bp[1:2], w[1:2], 0.0)
    out = x_ref[...] + mod_ref[5:6] * _dot_tn(sel_t, ybuf[slot])
    if final:
        out = _rms_rows(out) * fg_ref[...]
    o_ref[...] = out


def _moe_combine(y, runs, bufpos, x, wts, modl, tiles_per_batch, lat_only, final_g=None):
    t, d = x.shape
    n = t // TM
    off = 1 if lat_only else 0
    final = final_g is not None
    in_specs = [pl.BlockSpec(memory_space=pl.ANY),
                pl.BlockSpec((TM, d), lambda i, *_: (i, 0)),
                pl.BlockSpec((None, 2, TM), lambda i, *_: (i, 0, 0)),
                pl.BlockSpec((2, TM), lambda i, *_: (0, i)),
                pl.BlockSpec((None, None, 6, d),
                             lambda i, *_: (i // tiles_per_batch, jnp.minimum(i % tiles_per_batch + off, 1), 0, 0))]
    args = [y, x, bufpos, wts, modl]
    if final:
        in_specs.append(pl.BlockSpec((1, d), lambda i, *_: (0, 0)))
        args.append(final_g)
    grid_spec = pltpu.PrefetchScalarGridSpec(
        num_scalar_prefetch=2,
        grid=(n,),
        in_specs=in_specs,
        out_specs=pl.BlockSpec((TM, d), lambda i, *_: (i, 0)),
        scratch_shapes=[pltpu.VMEM((2, MOE_BUF, d), f32), pltpu.SemaphoreType.DMA((2,))],
    )
    return pl.pallas_call(
        functools.partial(_moe_combine_kernel, final=final),
        grid_spec=grid_spec,
        out_shape=jax.ShapeDtypeStruct((t, d), f32),
        compiler_params=_cparams(("arbitrary",)),
        name="moe_combine_final" if final else "moe_combine",
    )(*runs, *args)


def _moe(h, x, idx, wts, modl, experts, layer, tiles_per_batch, lat_only, final_g=None):
    runs, gaps, bufpos, tile_e, n_used, n_tiles = _dispatch_plan(idx)
    xs = _moe_dispatch(h, runs, gaps, bufpos, n_tiles)
    y = _moe_ffn(xs, tile_e, n_used, *experts, layer)
    return _moe_combine(y, runs, bufpos, x, wts, modl, tiles_per_batch, lat_only, final_g)


def _cd_in_kernel(x_ref, mod_ref, g_ref, w_ref, ag_ref, qkv_ref, dg_ref, ab_ref):
    mod = mod_ref[...]
    h = _rms_rows(x_ref[...]) * g_ref[...] * (1.0 + mod[1:2]) + mod[0:1]
    acc = _dot(h, w_ref[...])
    c1 = 2 * CV_CH
    c2 = c1 + DN_QKV
    c3 = c2 + DN_QK
    ag_ref[...] = acc[:, 0:c1]
    qkv_ref[...] = acc[:, c1:c2]
    dg_ref[...] = acc[:, c2:c3]
    ab_ref[...] = acc[:, c3:c3 + LANES]


def _cd_in_proj(x, modl, g, w):
    b, l, d = x.shape
    nt = l // TM
    n = w.shape[1]
    full = lambda shape: pl.BlockSpec(shape, lambda bi, i: (0,) * len(shape))
    widths = (2 * CV_CH, DN_QKV, DN_QK, LANES)
    return pl.pallas_call(
        _cd_in_kernel,
        grid=(b, nt),
        in_specs=[pl.BlockSpec((None, TM, d), lambda bi, i: (bi, i, 0)),
                  pl.BlockSpec((None, None, 6, d), _mod_index),
                  full((1, d)), full((d, n))],
        out_specs=[pl.BlockSpec((None, TM, w_), lambda bi, i: (bi, i, 0)) for w_ in widths],
        out_shape=[jax.ShapeDtypeStruct((b, l, w_), f32) for w_ in widths],
        compiler_params=_cparams(("arbitrary", "arbitrary")),
        name="cd_in_proj",
    )(x, modl, g, w)


CV_HALO = 16
CV_ROWS = 32


def _conformer_kernel(prev_ref, cur_ref, next_ref, w_ref, b_ref, lng_ref, lnb_ref, o_ref, ext_ref, sh_ref):
    i = pl.program_id(1)
    last = pl.num_programs(1) - 1

    def glu(z):
        return z[:, 0:CV_CH] * _sigmoid(z[:, CV_CH:2 * CV_CH])

    ext_ref[0:CV_HALO, :] = jnp.where(i > 0, glu(prev_ref[...]), 0.0)
    ext_ref[CV_HALO:CV_HALO + TM, :] = glu(cur_ref[...])
    ext_ref[CV_HALO + TM:2 * CV_HALO + TM, :] = jnp.where(i < last, glu(next_ref[...]), 0.0)
    n_sh = sh_ref.shape[1]
    for r in range(1, SUBLANES):
        sh_ref[r] = ext_ref[pl.ds(r, n_sh), :]
    base = CV_HALO - CV_WIDTH // 2
    groups = CV_ROWS // SUBLANES
    for rb in range(TM // CV_ROWS):
        acc = jnp.zeros((groups, SUBLANES, CV_CH), f32)
        for j in range(CV_WIDTH):
            a, r = divmod(base + j, SUBLANES)
            start = rb * CV_ROWS + a * SUBLANES
            x = ext_ref[pl.ds(start, CV_ROWS), :] if r == 0 else sh_ref[r, pl.ds(start, CV_ROWS), :]
            acc = acc + x.reshape(groups, SUBLANES, CV_CH) * w_ref[j][None]
        y = _layer_norm_rows(acc.reshape(CV_ROWS, CV_CH) + b_ref[...], lng_ref[...], lnb_ref[...])
        o_ref[rb * CV_ROWS:(rb + 1) * CV_ROWS, :] = _silu(y).astype(bf16)


def _conformer(ag, w, bias, lng, lnb):
    b, l, c2 = ag.shape
    nt = l // TM - 1
    hb = TM // CV_HALO
    n_halo = l // CV_HALO
    full = lambda shape: pl.BlockSpec(shape, lambda bi, i: (0,) * len(shape))
    return pl.pallas_call(
        _conformer_kernel,
        grid=(b, nt),
        in_specs=[pl.BlockSpec((None, CV_HALO, c2), lambda bi, i: (bi, (i + 1) * hb - 1, 0)),
                  pl.BlockSpec((None, TM, c2), lambda bi, i: (bi, i + 1, 0)),
                  pl.BlockSpec((None, CV_HALO, c2), lambda bi, i: (bi, jnp.minimum((i + 2) * hb, n_halo - 1), 0)),
                  full((CV_WIDTH, SUBLANES, CV_CH)), full((1, CV_CH)), full((1, CV_CH)), full((1, CV_CH))],
        out_specs=pl.BlockSpec((None, TM, CV_CH), lambda bi, i: (bi, i, 0)),
        out_shape=jax.ShapeDtypeStruct((b, nt * TM, CV_CH), bf16),
        scratch_shapes=[pltpu.VMEM((TM + 2 * CV_HALO, CV_CH), f32),
                        pltpu.VMEM((SUBLANES, TM + 2 * CV_HALO - SUBLANES, CV_CH), f32)],
        compiler_params=_cparams(("arbitrary", "arbitrary")),
        name="conformer",
    )(ag, ag, ag, w, bias, lng, lnb)


DN_HALO = 8


def _delta_feat_kernel(prev_ref, cur_ref, next_ref, ab_ref, w_ref, alog_ref, dtb_ref,
                       q_ref, k_ref, v_ref, g_ref, ext_ref):
    i = pl.program_id(1)
    last = pl.num_programs(1) - 1
    ext_ref[0:DN_HALO, :] = jnp.where(i > 1, prev_ref[...], 0.0)
    ext_ref[DN_HALO:DN_HALO + TM, :] = cur_ref[...]
    ext_ref[DN_HALO + TM:2 * DN_HALO + TM, :] = jnp.where((i > 0) & (i < last), next_ref[...], 0.0)
    w = w_ref[...]
    base = DN_HALO - DN_CONV // 2
    outs = (q_ref, k_ref, v_ref)
    for c in range(DN_QKV // LANES):
        cs = slice(c * LANES, (c + 1) * LANES)
        acc = jnp.zeros((TM, LANES), f32)
        for j in range(DN_CONV):
            acc = acc + ext_ref[pl.ds(base + j, TM), cs] * w[j:j + 1, cs]
        y = _silu(acc)
        part, h = divmod(c, DN_HEADS)
        if part < 2:
            y = y * lax.rsqrt(jnp.sum(y * y, axis=-1, keepdims=True) + EPS)
            if part == 0:
                y = y * (DN_DK ** -0.5)
        outs[part][:, h * LANES:(h + 1) * LANES] = y

    ab = ab_ref[...]
    g = -jnp.exp(alog_ref[...]) * (jnp.maximum(ab + dtb_ref[...], 0.0)
                                   + jnp.log(1.0 + jnp.exp(-jnp.abs(ab + dtb_ref[...]))))
    beta = _sigmoid(ab)
    lane = lax.broadcasted_iota(i32, ab.shape, 1)
    row = lax.broadcasted_iota(i32, (TM, TM), 0)
    col = lax.broadcasted_iota(i32, (TM, TM), 1)
    same = (row // DN_CHUNK) == (col // DN_CHUNK)
    for d in range(N_DIR):
        gd = g if d == 0 else pltpu.roll(g, LANES - DN_HEADS, 1)
        bd = pltpu.roll(beta, LANES - DN_HEADS * (d + 1), 1)
        tri = (same & ((col <= row) if d == 0 else (col >= row))).astype(f32)
        gc = jnp.dot(tri, jnp.where(lane < DN_HEADS, gd, 0.0), precision=lax.Precision.HIGHEST,
                     preferred_element_type=f32)
        g_ref[d] = jnp.where(lane < DN_HEADS, gc, bd)


def _delta_features(qkv, ab, w, alog, dtb):
    b, l, c = qkv.shape
    nt = l // TM
    hb = TM // DN_HALO
    n_halo = l // DN_HALO
    full = lambda shape: pl.BlockSpec(shape, lambda bi, i: (0,) * len(shape))
    return pl.pallas_call(
        _delta_feat_kernel,
        grid=(b, nt),
        in_specs=[pl.BlockSpec((None, DN_HALO, c), lambda bi, i: (bi, jnp.maximum(i * hb - 1, 0), 0)),
                  pl.BlockSpec((None, TM, c), lambda bi, i: (bi, i, 0)),
                  pl.BlockSpec((None, DN_HALO, c), lambda bi, i: (bi, jnp.minimum((i + 1) * hb, n_halo - 1), 0)),
                  pl.BlockSpec((None, TM, LANES), lambda bi, i: (bi, i, 0)),
                  full((DN_CONV, c)), full((1, LANES)), full((1, LANES))],
        out_specs=[pl.BlockSpec((None, TM, DN_QK), lambda bi, i: (bi, i, 0))] * 3
        + [pl.BlockSpec((N_DIR, None, TM, LANES), lambda bi, i: (0, bi, i, 0))],
        out_shape=[jax.ShapeDtypeStruct((b, l, DN_QK), f32)] * 3
        + [jax.ShapeDtypeStruct((N_DIR, b, l, LANES), f32)],
        scratch_shapes=[pltpu.VMEM((TM + 2 * DN_HALO, c), f32)],
        compiler_params=_cparams(("arbitrary", "arbitrary")),
        name="delta_features",
    )(qkv, qkv, qkv, ab, w, alog, dtb)


DN_PRE_CHUNKS = 2
DN_EG_ROWS = 8


def _delta_pre_kernel(q_ref, k_ref, v_ref, g_ref, u_ref, w_ref, qe_ref, kd_ref, qk_ref, eg_ref):
    n = DN_CHUNK
    row = lax.broadcasted_iota(i32, (n, LANES), 0)
    col = lax.broadcasted_iota(i32, (n, LANES), 1)
    eye = row == col
    valid = col < n
    zpad = jnp.zeros((LANES - n, LANES), f32)
    sh = [(s, h) for s in range(DN_PRE_CHUNKS) for h in range(DN_HEADS)]
    rsl = lambda s: slice(s * n, (s + 1) * n)
    csl = lambda h: slice(h * LANES, (h + 1) * LANES)
    q = [q_ref[rsl(s), csl(h)] for s, h in sh]
    k = [k_ref[rsl(s), csl(h)] for s, h in sh]
    v = [v_ref[rsl(s), csl(h)] for s, h in sh]
    k_pad = [jnp.concatenate([ki, zpad], axis=0) for ki in k]
    kk = [_dot_nt(ki, kp) for ki, kp in zip(k, k_pad)]
    qk = [_dot_nt(qi, kp) for qi, kp in zip(q, k_pad)]
    prob = [(i, d) for i in range(len(sh)) for d in range(N_DIR)]
    gc, beta, gl, decay, a = [], [], [], [], []
    for i, d in prob:
        s, h = sh[i]
        gates = g_ref[d, rsl(s), :]
        gci = gates[:, h:h + 1]
        bi = gates[:, DN_HEADS + h:DN_HEADS + h + 1]
        last = n - 1 if d == 0 else 0
        incl = ((col <= row) if d == 0 else (col >= row)) & valid
        strict = ((col < row) if d == 0 else (col > row)) & valid
        gc_row = jnp.sum(jnp.where(eye, gci, 0.0), axis=0, keepdims=True)
        di = jnp.exp(jnp.where(incl, gci - gc_row, -jnp.inf))
        gc.append(gci)
        beta.append(bi)
        gl.append(gates[last:last + 1, h:h + 1])
        decay.append(di)
        a.append(jnp.where(strict, -(bi * kk[i] * di), 0.0))
    y = list(a)
    p = list(a)
    for _ in range(5):
        p = [_dot(pi[:, 0:n], pi) for pi in p]
        yp = [_dot(yi[:, 0:n], pi) for yi, pi in zip(y, p)]
        y = [yi + pi + ypi for yi, pi, ypi in zip(y, p, yp)]
    e_gc = [jnp.exp(g) for g in gc]
    rhs = [jnp.concatenate([v[i] * beta[j], k[i] * (beta[j] * e_gc[j])], axis=1) for j, (i, d) in enumerate(prob)]
    sol = [r + _dot(yi[:, 0:n], r) for yi, r in zip(y, rhs)]
    for j, (i, d) in enumerate(prob):
        s, h = sh[i]
        rs, cs = rsl(s), csl(h)
        u_ref[d, rs, cs] = sol[j][:, 0:LANES]
        w_ref[d, rs, cs] = sol[j][:, LANES:2 * LANES].astype(bf16)
        qe_ref[d, rs, cs] = (q[i] * e_gc[j]).astype(bf16)
        kd_ref[d, rs, cs] = (k[i] * jnp.exp(gl[j] - gc[j])).astype(bf16)
        qk_ref[d, rs, cs] = (qk[i] * decay[j]).astype(bf16)
    for s in range(DN_PRE_CHUNKS):
        for d in range(N_DIR):
            last = s * n + (n - 1 if d == 0 else 0)
            eg_ref[d, s * DN_EG_ROWS:(s + 1) * DN_EG_ROWS, :] = jnp.broadcast_to(
                jnp.exp(g_ref[d, last:last + 1, :]), (DN_EG_ROWS, LANES))


def _delta_pre(q, k, v, gates):
    b, l, c = q.shape
    rows = DN_PRE_CHUNKS * DN_CHUNK
    n = l // rows
    row = pl.BlockSpec((None, rows, c), lambda bi, i: (bi, i, 0))
    out = pl.BlockSpec((N_DIR, None, rows, c), lambda bi, i: (0, bi, i, 0))
    eg_rows = DN_PRE_CHUNKS * DN_EG_ROWS
    return pl.pallas_call(
        _delta_pre_kernel,
        grid=(b, n),
        in_specs=[row, row, row, pl.BlockSpec((N_DIR, None, rows, LANES), lambda bi, i: (0, bi, i, 0))],
        out_specs=[out] * 5 + [pl.BlockSpec((N_DIR, None, eg_rows, LANES), lambda bi, i: (0, bi, i, 0))],
        out_shape=[jax.ShapeDtypeStruct((N_DIR, b, l, c), f32)]
        + [jax.ShapeDtypeStruct((N_DIR, b, l, c), bf16)] * 4
        + [jax.ShapeDtypeStruct((N_DIR, b, n * eg_rows, LANES), f32)],
        compiler_params=_cparams(("arbitrary", "arbitrary")),
        name="delta_pre",
    )(q, k, v, gates)


def _delta_scan_kernel(*refs):
    n_in = 6 * N_DIR
    o_refs = refs[n_in:n_in + N_DIR]
    s_ref = refs[n_in + N_DIR]
    n = DN_CHUNK
    n_sub = TM // DN_CHUNK

    @pl.when(pl.program_id(1) == 0)
    def _():
        s_ref[...] = jnp.zeros(s_ref.shape, f32)

    dh = [(d, h) for d in range(N_DIR) for h in range(DN_HEADS)]
    csl = lambda h: slice(h * LANES, (h + 1) * LANES)
    ins = lambda d: refs[6 * d:6 * d + 6]
    s = [s_ref[d, h] for d, h in dh]
    for c in range(n_sub):
        sub = lambda d: c if d == 0 else n_sub - 1 - c
        rsl = lambda d: slice(sub(d) * n, (sub(d) + 1) * n)
        r = [_dot(jnp.concatenate([ins(d)[1][rsl(d), csl(h)], ins(d)[2][rsl(d), csl(h)]], axis=0), si)
             for (d, h), si in zip(dh, s)]
        v_new = [ins(d)[0][rsl(d), csl(h)] - ri[0:n] for (d, h), ri in zip(dh, r)]
        s_add = [_dot_tn(ins(d)[3][rsl(d), csl(h)], vi) for (d, h), vi in zip(dh, v_new)]
        o_add = [_dot(ins(d)[4][rsl(d), h * LANES:h * LANES + n], vi) for (d, h), vi in zip(dh, v_new)]
        for j, (d, h) in enumerate(dh):
            o_refs[d][rsl(d), csl(h)] = r[j][n:2 * n] + o_add[j]
        s = [s[j] * ins(d)[5][sub(d) * DN_EG_ROWS:sub(d) * DN_EG_ROWS + 1, h:h + 1] + s_add[j]
             for j, (d, h) in enumerate(dh)]
    for j, (d, h) in enumerate(dh):
        s_ref[d, h] = s[j]


def _delta_scan(u, w, qe, kd, qk, eg):
    _, b, l, c = u.shape
    nt = l // TM
    eg_rows = TM // DN_CHUNK * DN_EG_ROWS

    def tile(d, ti):
        if d == 0:
            return ti
        return jnp.where(ti == 0, 0, nt - ti)

    in_specs, args = [], []
    for d in range(N_DIR):
        spec = pl.BlockSpec((None, None, TM, c), lambda bi, ti, d=d: (d, bi, tile(d, ti), 0))
        in_specs += [spec] * 5 + [pl.BlockSpec((None, None, eg_rows, LANES),
                                               lambda bi, ti, d=d: (d, bi, tile(d, ti), 0))]
        args += [u, w, qe, kd, qk, eg]
    return pl.pallas_call(
        _delta_scan_kernel,
        grid=(b, nt),
        in_specs=in_specs,
        out_specs=[pl.BlockSpec((None, TM, c), lambda bi, ti, d=d: (bi, tile(d, ti), 0))
                   for d in range(N_DIR)],
        out_shape=[jax.ShapeDtypeStruct((b, l, c), f32)] * N_DIR,
        scratch_shapes=[pltpu.VMEM((N_DIR, DN_HEADS, DN_DK, LANES), f32)],
        compiler_params=_cparams(("arbitrary", "arbitrary")),
        name="delta_scan",
    )(*args)


def _rope_tables(s, ctx_len):
    rows = s // GRID_W
    row = jnp.repeat(jnp.arange(rows, dtype=i32), GRID_W)
    col = jnp.tile(jnp.arange(GRID_W, dtype=i32), rows)
    freqs = ROPE_THETA ** (-jnp.arange(ROPE_PAIRS, dtype=f32) / ROPE_PAIRS)
    ang = jnp.stack([row, col], axis=-1).astype(f32)[..., None] * freqs
    cos = jnp.cos(ang)
    sin = jnp.sin(ang)
    cos_h = jnp.concatenate([cos[:, 0], cos[:, 0], cos[:, 1], cos[:, 1]], axis=-1)
    sin_h = jnp.concatenate([-sin[:, 0], sin[:, 0], -sin[:, 1], sin[:, 1]], axis=-1)
    cos_t = jnp.concatenate([jnp.ones((ctx_len, HEAD_DIM), f32), cos_h], axis=0)
    sin_t = jnp.concatenate([jnp.zeros((ctx_len, HEAD_DIM), f32), sin_h], axis=0)
    return jnp.tile(cos_t, (1, 2)), jnp.tile(sin_t, (1, 2))


def _pad_lanes(v, n=LANES):
    v = v.reshape(1, -1)
    return jnp.pad(v, ((0, 0), (0, n - v.shape[1])))


def kernel(x, c, ctx, c_ctx, mod_w, mod_b, norm1_g, norm2_g, ab_w_in, ab_q_norm, ab_k_norm, gm_ln_g, gm_ln_b, gm_w_s, gm_b_s, ab_w_out, cd_w_in, cv_dw_w, cv_dw_b, cv_ln_g, cv_ln_b, dn_conv_w, dn_a_log, dn_dt_bias, dn_o_norm, cd_w_out, router_w, router_b, moe_w_gate, moe_w_up, moe_w_down, final_norm_g):
    b, s, d = x.shape
    ctx_len = ctx.shape[1]
    assert ctx_len == TM and s % (4 * TM) == 0 and mod_w.shape[0] == 2
    l = ctx_len + s
    nt = l // TM

    n_rows = -(-(b + 1) // 8) * 8
    cc = jnp.concatenate([c, c_ctx[None, :], jnp.zeros((n_rows - b - 1, d), f32)], axis=0)
    mod = _modulation(cc, mod_w, mod_b)
    mod_lat = mod[:, :b].reshape(2, b, 1, 6, d)
    mod_ctx = jnp.broadcast_to(mod[:, b].reshape(2, 1, 1, 6, d), (2, b, 1, 6, d))
    modl = jnp.concatenate([mod_ctx, mod_lat], axis=2)

    rw32 =jnp.pad(router_w, ((0, 0), (0, LANES - N_EXPERTS)))
    rw_hi = rw32.astype(bf16)
    rw = jnp.concatenate([rw_hi, (rw32 - rw_hi.astype(f32)).astype(bf16)], axis=1)
    rb = router_b.reshape(N_EXPERTS, 1)
    experts = (moe_w_gate, moe_w_up, moe_w_down)

    cos_t, sin_t = _rope_tables(s, ctx_len)
    lane = jnp.arange(2 * LANES)
    seg = ((lane[:, None] // HEAD_DIM) == (lane[None, :] // HEAD_DIM)).astype(bf16) * (1.0 / HEAD_DIM)
    q, kd, vt, sg = _ab_in_proj(
        ctx, x, modl[0], norm1_g[0:1], ab_w_in[0].astype(bf16), seg,
        jnp.tile(ab_q_norm[0], 2)[None, :], jnp.tile(ab_k_norm[0], 2)[None, :], cos_t, sin_t,
        gm_ln_g[0:1], gm_ln_b[0:1], gm_w_s[0].astype(bf16),
        jnp.broadcast_to(gm_b_s[0][:, :, None], (GM_GROUPS, GM_CHUNK, LANES)))
    att = _attention(q, kd, vt)
    x1, h1, idx, wts = _out_proj(att, sg, (ctx, x), modl[0], norm2_g[0:1], ab_w_out[0].astype(bf16), rw, rb,
                                 delta=False)
    xs = _moe(h1, x1, idx, wts, modl[0], experts, 0, nt, False).reshape(b, l, d)

    c1 = 2 * CV_CH
    c2 = c1 + DN_QKV
    c3 = c2 + 2 * N_DIR * DN_HEADS
    w_in = cd_w_in[0]
    w_cd = jnp.concatenate([w_in[:, 0:c2], w_in[:, c3:], w_in[:, c2:c3],
                            jnp.zeros((d, LANES - (c3 - c2)), f32)], axis=1).astype(bf16)
    ag, qkv, dg, ab = _cd_in_proj(xs, modl[1], norm1_g[1:2], w_cd)
    conv_w = jnp.broadcast_to(cv_dw_w[0][:, None, :], (CV_WIDTH, SUBLANES, CV_CH))
    conv = _conformer(ag, conv_w, cv_dw_b[0:1], cv_ln_g[0:1], cv_ln_b[0:1])
    qn, kn, vv, gates = _delta_features(qkv, ab, dn_conv_w[0], _pad_lanes(dn_a_log[0]), _pad_lanes(dn_dt_bias[0]))
    o_f, o_b = _delta_scan(*_delta_pre(qn, kn, vv, gates))
    x2, h2, idx, wts = _out_proj(conv, (o_f, o_b, dg, dn_o_norm[0:1]), xs, modl[1], norm2_g[1:2],
                                 cd_w_out[0].astype(bf16), rw, rb, delta=True)
    out = _moe(h2, x2, idx, wts, modl[1], experts, 1, nt - 1, True, final_norm_g[None, :])
    return out.reshape(b, s, d)
```

```python
import functools
import math

import jax
import jax.numpy as jnp
from jax import lax
from jax.experimental import pallas as pl
from jax.experimental.pallas import tpu as pltpu

f32 = jnp.float32
bf16 = jnp.bfloat16
i32 = jnp.int32

GRID_W = 64
EPS = 1e-6
ATT_HEADS = 8
ATT_KV_HEADS = 2
HEAD_DIM = 64
ROPE_PAIRS = HEAD_DIM // 4
ROPE_THETA = 10000.0
A_Q = ATT_HEADS * HEAD_DIM
A_KV = ATT_KV_HEADS * HEAD_DIM
GM_GROUPS = 4
GM_CHUNK = 128
GM_CH = 512
CV_CH = 512
CV_WIDTH = 31
DN_HEADS = 4
DN_DK = 128
DN_CONV = 5
DN_CHUNK = 64
N_DIR = 2
DN_QK = DN_HEADS * DN_DK
DN_QKV = 3 * DN_QK
N_EXPERTS = 16
N_GROUPS = 4
EXPERT_FF = 512

LANES = 128
SUBLANES = 8
TM = 256
V_ROWS = HEAD_DIM + 16
VMEM_LIMIT = 48 * 1024 * 1024


def _cparams(sem):
    return pltpu.CompilerParams(dimension_semantics=sem, vmem_limit_bytes=VMEM_LIMIT)


def _sigmoid(x):
    return 1.0 / (1.0 + jnp.exp(-x))


def _silu(x):
    return x * _sigmoid(x)


def _gelu(x):
    return x * (0.5 * (1.0 + jnp.tanh(math.sqrt(2.0 / math.pi) * (x + 0.044715 * (x * x * x)))))


def _rms_rows(x):
    return x * lax.rsqrt(jnp.mean(x * x, axis=-1, keepdims=True) + EPS)


def _layer_norm_rows(x, g, b):
    mu = jnp.mean(x, axis=-1, keepdims=True)
    xc = x - mu
    var = jnp.mean(xc * xc, axis=-1, keepdims=True)
    return xc * lax.rsqrt(var + EPS) * g + b


def _dot(a, b):
    return jnp.dot(a.astype(bf16), b.astype(bf16), preferred_element_type=f32)


def _dot_nt(a, b):
    return lax.dot_general(a.astype(bf16), b.astype(bf16), (((1,), (1,)), ((), ())), preferred_element_type=f32)


def _dot_tn(a, b):
    return lax.dot_general(a.astype(bf16), b.astype(bf16), (((0,), (0,)), ((), ())), preferred_element_type=f32)


def _mod_kernel(c_ref, w_ref, b_ref, o_ref):
    c = c_ref[...]
    o_ref[...] = jnp.dot(_silu(c), w_ref[...], precision=lax.Precision.HIGHEST,
                         preferred_element_type=f32) + b_ref[...]


def _modulation(cc, mod_w, mod_b):
    depth, d, n = mod_w.shape
    tn = 1024
    rows = cc.shape[0]
    return pl.pallas_call(
        _mod_kernel,
        grid=(depth, n // tn),
        in_specs=[pl.BlockSpec((rows, d), lambda l, j: (0, 0)),
                  pl.BlockSpec((None, d, tn), lambda l, j: (l, 0, j)),
                  pl.BlockSpec((None, 1, tn), lambda l, j: (l, 0, j))],
        out_specs=pl.BlockSpec((None, rows, tn), lambda l, j: (l, 0, j)),
        out_shape=jax.ShapeDtypeStruct((depth, rows, n), f32),
        compiler_params=_cparams(("arbitrary", "arbitrary")),
        name="modulation",
    )(cc, mod_w, mod_b.reshape(depth, 1, n))


def _mod_index(b, i):
    return (b, jnp.minimum(i, 1), 0, 0)


def _head_norm_rope(x, seg_mean, gain, cos, sin, scale):
    xx = x * x
    outs = []
    lane = lax.broadcasted_iota(i32, (x.shape[0], LANES), 1)
    first = (lane % 32) < 16
    for c0 in range(0, x.shape[1], 2 * LANES):
        w = min(2 * LANES, x.shape[1] - c0)
        ms = _dot(xx[:, c0:c0 + w], seg_mean[0:w, 0:w])
        xn = x[:, c0:c0 + w] * lax.rsqrt(ms + EPS)
        for c1 in range(0, w, LANES):
            y = xn[:, c1:c1 + LANES] * gain
            partner = jnp.where(first, pltpu.roll(y, LANES - 16, 1), pltpu.roll(y, 16, 1))
            outs.append((y * cos + partner * sin) * scale)
    return outs


def _tile_rows(ctx_ref, x_ref):
    return jnp.where(pl.program_id(1) == 0, ctx_ref[...], x_ref[...])


def _ctx_lat_specs(d):
    return [pl.BlockSpec((None, TM, d), lambda bi, i: (bi, 0, 0)),
            pl.BlockSpec((None, TM, d), lambda bi, i: (bi, jnp.maximum(i - 1, 0), 0))]


def _ab_in_kernel(ctx_ref, x_ref, mod_ref, g_ref, w_ref, seg_ref, qg_ref, kg_ref, cos_ref, sin_ref,
                  lng_ref, lnb_ref, ws_ref, bs_ref, q_ref, kd_ref, vt_ref, s_ref):
    mod = mod_ref[...]
    h = _rms_rows(_tile_rows(ctx_ref, x_ref)) * g_ref[...] * (1.0 + mod[1:2]) + mod[0:1]
    acc = _dot(h, w_ref[...])
    cos = cos_ref[...]
    sin = sin_ref[...]
    seg = seg_ref[...]

    q = _head_norm_rope(acc[:, 0:A_Q], seg, qg_ref[...], cos, sin, HEAD_DIM ** -0.5 * math.log2(math.e))
    for c, qc in enumerate(q):
        q_ref[:, c * LANES:(c + 1) * LANES] = qc.astype(bf16)

    (k,) = _head_norm_rope(acc[:, A_Q:A_Q + A_KV], seg, kg_ref[...], cos, sin, 1.0)
    lane = lax.broadcasted_iota(i32, k.shape, 1)
    swapped = pltpu.roll(k, HEAD_DIM, 1)
    kd_ref[0] = jnp.where(lane < HEAD_DIM, k, swapped).astype(bf16)
    kd_ref[1] = jnp.where(lane < HEAD_DIM, swapped, k).astype(bf16)

    vt = acc[:, A_Q + A_KV:A_Q + 2 * A_KV].T
    ones = jnp.ones((V_ROWS - HEAD_DIM, vt.shape[1]), bf16)
    for j in range(ATT_KV_HEADS):
        vt_ref[j, 0:HEAD_DIM, :] = vt[j * HEAD_DIM:(j + 1) * HEAD_DIM].astype(bf16)
        vt_ref[j, HEAD_DIM:V_ROWS, :] = ones

    c0 = A_Q + 2 * A_KV
    gu = _gelu(acc[:, c0:c0 + GM_CH])
    ln = _layer_norm_rows(_gelu(acc[:, c0 + GM_CH:c0 + 2 * GM_CH]), lng_ref[...], lnb_ref[...])
    for n in range(acc.shape[0] // GM_CHUNK):
        r = slice(n * GM_CHUNK, (n + 1) * GM_CHUNK)
        for g in range(GM_GROUPS):
            cs = slice(g * LANES, (g + 1) * LANES)
            mixed = _dot(ws_ref[g], ln[r, cs]) + bs_ref[g]
            s_ref[r, cs] = (gu[r, cs] * mixed).astype(bf16)


def _ab_in_proj(ctx, x, modl, g, w, seg, qg, kg, cos, sin, lng, lnb, ws, bs):
    b, s, d = x.shape
    l = s + ctx.shape[1]
    nt = l // TM
    n = w.shape[1]
    full = lambda shape: pl.BlockSpec(shape, lambda bi, i: (0,) * len(shape))
    return pl.pallas_call(
        _ab_in_kernel,
        grid=(b, nt),
        in_specs=_ctx_lat_specs(d) + [
                  pl.BlockSpec((None, None, 6, d), _mod_index),
                  full((1, d)), full((d, n)), full((2 * LANES, 2 * LANES)),
                  full((1, LANES)), full((1, LANES)),
                  pl.BlockSpec((TM, LANES), lambda bi, i: (i, 0)),
                  pl.BlockSpec((TM, LANES), lambda bi, i: (i, 0)),
                  full((1, GM_CH)), full((1, GM_CH)),
                  full((GM_GROUPS, GM_CHUNK, GM_CHUNK)), full((GM_GROUPS, GM_CHUNK, LANES))],
        out_specs=[pl.BlockSpec((None, TM, A_Q), lambda bi, i: (bi, i, 0)),
                   pl.BlockSpec((None, ATT_KV_HEADS, TM, LANES), lambda bi, i: (bi, 0, i, 0)),
                   pl.BlockSpec((None, ATT_KV_HEADS, None, V_ROWS, TM), lambda bi, i: (bi, 0, i, 0, 0)),
                   pl.BlockSpec((None, TM, GM_CH), lambda bi, i: (bi, i, 0))],
        out_shape=[jax.ShapeDtypeStruct((b, l, A_Q), bf16),
                   jax.ShapeDtypeStruct((b, ATT_KV_HEADS, l, LANES), bf16),
                   jax.ShapeDtypeStruct((b, ATT_KV_HEADS, nt, V_ROWS, TM), bf16),
                   jax.ShapeDtypeStruct((b, l, GM_CH), bf16)],
        compiler_params=_cparams(("arbitrary", "arbitrary")),
        name="ab_in_proj",
    )(ctx, x, modl, g, w, seg, qg, kg, cos, sin, lng, lnb, ws, bs)


def _attn_kernel(q_ref, k_ref, vt_ref, o_ref, acc_ref, m_ref, st_ref, st2_ref, *, n_pairs):
    i = pl.program_id(2)
    q = q_ref[...]
    lane = lax.broadcasted_iota(i32, (TM, LANES), 1)
    zero = jnp.zeros((TM, LANES), bf16)
    qs = []
    for p in range(2):
        qp = q[:, p * LANES:(p + 1) * LANES]
        qs.append(jnp.where(lane < HEAD_DIM, qp, zero))
        qs.append(jnp.where(lane < HEAD_DIM, zero, qp))
    heads = range(len(qs))
    hsl = lambda h: slice(h * TM, (h + 1) * TM)
    m_ref[...] = jnp.full(m_ref.shape, -jnp.inf, f32)
    acc_ref[...] = jnp.zeros(acc_ref.shape, f32)

    def scores(kb, h):
        return _dot_nt(kb, qs[h])

    def update(st, vb, h):
        m_old = m_ref[:, hsl(h)]
        m_new = jnp.maximum(m_old, jnp.max(st, axis=0, keepdims=True))
        p = jnp.exp2(st - m_new).astype(bf16)
        acc_ref[:, hsl(h)] = (acc_ref[:, hsl(h)] * jnp.exp2(m_old - m_new)
                              + jnp.dot(vb, p, preferred_element_type=f32))
        m_ref[:, hsl(h)] = m_new

    def latent_keys(t):
        return k_ref[pl.ds(pl.multiple_of(TM + t * (2 * TM), TM), 2 * TM), :]

    def latent_values(t):
        return jnp.concatenate([vt_ref[1 + 2 * t], vt_ref[2 + 2 * t]], axis=1)

    kb0 = k_ref[0:TM, :]

    @pl.when(i == 0)
    def _():
        for h in heads:
            update(scores(kb0, h), vt_ref[0], h)

    @pl.when(i > 0)
    def _():
        def step(t, cur, nxt):
            kb_next = latent_keys(t + 1)
            vb = latent_values(t)
            for h in heads:
                nxt[:, hsl(h)] = scores(kb_next, h)
                update(cur[:, hsl(h)], vb, h)

        kb = latent_keys(0)
        st_c = [scores(kb0, h) for h in heads]
        for h in heads:
            st_ref[:, hsl(h)] = scores(kb, h)
            update(st_c[h], vt_ref[0], h)

        for j in range((n_pairs - 2) // 2):
            step(2 * j, st_ref, st2_ref)
            step(2 * j + 1, st2_ref, st_ref)
        step(n_pairs - 2, st_ref, st2_ref)
        vb = latent_values(n_pairs - 1)
        for h in heads:
            update(st2_ref[:, hsl(h)], vb, h)

    acc = acc_ref[...]
    o = acc[0:HEAD_DIM] / acc[HEAD_DIM:HEAD_DIM + 1]
    for p in range(2):
        pair = jnp.concatenate([o[:, (2 * p) * TM:(2 * p + 1) * TM],
                                o[:, (2 * p + 1) * TM:(2 * p + 2) * TM]], axis=0)
        o_ref[:, p * LANES:(p + 1) * LANES] = pair.T.astype(bf16)


def _attention(q, kd, vt):
    b, l, _ = q.shape
    nt = l // TM
    group_w = A_Q // ATT_KV_HEADS
    assert (l - TM) % (4 * TM) == 0
    return pl.pallas_call(
        functools.partial(_attn_kernel, n_pairs=(l - TM) // (2 * TM)),
        grid=(b, ATT_KV_HEADS, nt),
        in_specs=[pl.BlockSpec((None, TM, group_w), lambda bi, j, i: (bi, i, j)),
                  pl.BlockSpec((None, None, l, LANES), lambda bi, j, i: (bi, j, 0, 0)),
                  pl.BlockSpec((None, None, nt, V_ROWS, TM), lambda bi, j, i: (bi, j, 0, 0, 0))],
        out_specs=pl.BlockSpec((None, TM, group_w), lambda bi, j, i: (bi, i, j)),
        out_shape=jax.ShapeDtypeStruct((b, l, A_Q), bf16),
        scratch_shapes=[pltpu.VMEM((V_ROWS, 4 * TM), f32), pltpu.VMEM((1, 4 * TM), f32),
                        pltpu.VMEM((2 * TM, 4 * TM), f32), pltpu.VMEM((2 * TM, 4 * TM), f32)],
        compiler_params=_cparams(("arbitrary", "arbitrary", "arbitrary")),
        name="attention",
    )(q, kd, vt)


def _route_rows(scores, sel):
    s = [sel[e:e + 1] for e in range(N_EXPERTS)]
    sc = [scores[e:e + 1] for e in range(N_EXPERTS)]
    per = N_EXPERTS // N_GROUPS
    gs = []
    for g in range(N_GROUPS):
        a, b, c, d = s[per * g:per * g + per]
        hi1, lo1 = jnp.maximum(a, b), jnp.minimum(a, b)
        hi2, lo2 = jnp.maximum(c, d), jnp.minimum(c, d)
        gs.append(jnp.maximum(hi1, hi2) + jnp.maximum(jnp.minimum(hi1, hi2), jnp.maximum(lo1, lo2)))
    best = jnp.zeros(gs[0].shape, i32)
    best_v = gs[0]
    for g in range(1, N_GROUPS):
        better = gs[g] > best_v
        best = jnp.where(better, g, best)
        best_v = jnp.where(better, gs[g], best_v)
    v, w = [], []
    for j in range(per):
        vj, wj = s[j], sc[j]
        for g in range(1, N_GROUPS):
            vj = jnp.where(best == g, s[per * g + j], vj)
            wj = jnp.where(best == g, sc[per * g + j], wj)
        v.append(vj)
        w.append(wj)
    i1 = jnp.zeros(best.shape, i32)
    m1, w1 = v[0], w[0]
    for j in range(1, per):
        better = v[j] > m1
        i1 = jnp.where(better, j, i1)
        m1 = jnp.where(better, v[j], m1)
        w1 = jnp.where(better, w[j], w1)
    i2 = jnp.zeros(best.shape, i32)
    m2 = jnp.full(m1.shape, -jnp.inf, f32)
    w2 = jnp.zeros(m1.shape, f32)
    for j in range(per):
        cand = jnp.where(i1 == j, -jnp.inf, v[j])
        better = cand > m2
        i2 = jnp.where(better, j, i2)
        m2 = jnp.where(better, cand, m2)
        w2 = jnp.where(better, w[j], w2)
    tot = w1 + w2
    return best * per + i1, best * per + i2, w1 / tot, w2 / tot


def _out_kernel(*refs, delta):
    if delta:
        (a1_ref, of_ref, ob_ref, dg_ref, on_ref, x_ref, mod_ref, g_ref, w_ref, rw_ref, rb_ref,
         xo_ref, h_ref, idx_ref, wt_ref) = refs
        o = of_ref[...] + ob_ref[...]
        dg = dg_ref[...]
        parts = []
        for h in range(DN_HEADS):
            cs = slice(h * LANES, (h + 1) * LANES)
            parts.append(_rms_rows(o[:, cs]) * on_ref[...] * _silu(dg[:, cs]))
        a2 = jnp.concatenate(parts, axis=1)
        x = x_ref[...]
    else:
        (a1_ref, a2_ref, ctx_ref, x_ref, mod_ref, g_ref, w_ref, rw_ref, rb_ref,
         xo_ref, h_ref, idx_ref, wt_ref) = refs
        a2 = a2_ref[...]
        x = _tile_rows(ctx_ref, x_ref)
    half = a1_ref.shape[-1]
    y = _dot(a1_ref[...], w_ref[0:half, :]) + _dot(a2, w_ref[half:2 * half, :])
    mod = mod_ref[...]
    xn = x + mod[2:3] * y
    xo_ref[...] = xn
    h2 = _rms_rows(xn) * g_ref[...] * (1.0 + mod[4:5]) + mod[3:4]
    h_ref[...] = h2.astype(bf16)
    h_hi = h2.astype(bf16)
    h_lo = (h2 - h_hi.astype(f32)).astype(bf16)
    part = jnp.dot(h_hi, rw_ref[...], preferred_element_type=f32)
    logits = (part[:, 0:LANES] + part[:, LANES:2 * LANES]
              + jnp.dot(h_lo, rw_ref[:, 0:LANES], preferred_element_type=f32))
    scores = _sigmoid(logits.T[0:N_EXPERTS])
    e1, e2, w1, w2 = _route_rows(scores, scores + rb_ref[...])
    idx_ref[0:1, :] = e1
    idx_ref[1:2, :] = e2
    wt_ref[0:1, :] = w1
    wt_ref[1:2, :] = w2


def _out_proj(a1, a2s, x, modl, g, w, rw, rb, *, delta):
    if delta:
        b, l, d = x.shape
        x_specs, x_args, off = [pl.BlockSpec((None, TM, d), lambda bi, i: (bi, i + 1, 0))], [x], 1
    else:
        b, s, d = x[1].shape
        l = s + x[0].shape[1]
        x_specs, x_args, off = _ctx_lat_specs(d), list(x), 0
    nt = l // TM
    nto = nt - off
    half = a1.shape[-1]
    full = lambda shape: pl.BlockSpec(shape, lambda bi, i: (0,) * len(shape))
    row_l = lambda w_: pl.BlockSpec((None, TM, w_), lambda bi, i: (bi, i + off, 0))
    if delta:
        a2_specs = [row_l(half), row_l(half), row_l(half), full((1, LANES))]
        a2_args = list(a2s)
        a1_spec = pl.BlockSpec((None, TM, half), lambda bi, i: (bi, i, 0))
    else:
        a2_specs = [row_l(half)]
        a2_args = [a2s]
        a1_spec = row_l(half)
    t_out = b * nto * TM
    flat = lambda bi, i: (bi * nto + i, 0)
    return pl.pallas_call(
        functools.partial(_out_kernel, delta=delta),
        grid=(b, nto),
        in_specs=[a1_spec] + a2_specs + x_specs + [
            pl.BlockSpec((None, None, 6, d), lambda bi, i: (bi, jnp.minimum(i + off, 1), 0, 0)),
            full((1, d)), full((2 * half, d)), full((d, 2 * LANES)), full((N_EXPERTS, 1))],
        out_specs=[pl.BlockSpec((TM, d), flat), pl.BlockSpec((TM, d), flat),
                   pl.BlockSpec((2, TM), lambda bi, i: (0, bi * nto + i)),
                   pl.BlockSpec((2, TM), lambda bi, i: (0, bi * nto + i))],
        out_shape=[jax.ShapeDtypeStruct((t_out, d), f32), jax.ShapeDtypeStruct((t_out, d), bf16),
                   jax.ShapeDtypeStruct((2, t_out), i32), jax.ShapeDtypeStruct((2, t_out), f32)],
        compiler_params=_cparams(("arbitrary", "arbitrary")),
        name="out_proj_delta" if delta else "out_proj",
    )(a1, *a2_args, *x_args, modl, g, w, rw, rb)


MOE_GROUP = 8
MOE_BUF = 2 * TM + N_EXPERTS * MOE_GROUP
MOE_GROUPS = MOE_BUF // MOE_GROUP


def _dispatch_plan(idx):
    t = idx.shape[1]
    nt = t // TM
    e = idx.reshape(2, nt, TM).transpose(1, 0, 2).reshape(nt, 2 * TM)
    onehot = (e[:, :, None] == jnp.arange(N_EXPERTS, dtype=i32)).astype(i32)
    lrank = jnp.cumsum(onehot, axis=1) - onehot
    cnt = jnp.sum(onehot, axis=1)
    cnt_g = (cnt + MOE_GROUP - 1) // MOE_GROUP * MOE_GROUP
    boff = jnp.cumsum(cnt_g, axis=1) - cnt_g
    bufpos = jnp.sum((boff[:, None, :] + lrank) * onehot, axis=2)
    region = jnp.sum(cnt_g, axis=0)
    padded = (region + TM - 1) // TM * TM
    ends = jnp.cumsum(padded)
    base = ends - padded
    gstart = base[None, :] + jnp.cumsum(cnt_g, axis=0) - cnt_g
    n_tiles = -(-(2 * t + nt * N_EXPERTS * (MOE_GROUP - 1)) // TM) + N_EXPERTS
    n_used = (ends[-1] // TM).astype(i32)
    tiles = jnp.arange(n_tiles, dtype=i32)
    tile_e = jnp.sum((tiles[:, None] * TM >= ends[None, :]).astype(i32), axis=1)
    last_e = jnp.sum(((n_used - 1) * TM >= ends).astype(i32))
    tile_e = jnp.where(tiles < n_used, tile_e, last_e)
    g_row = jnp.arange(MOE_GROUPS, dtype=i32) * MOE_GROUP
    g_exp = jnp.sum((g_row[None, :, None] >= (boff + cnt_g)[:, None, :]).astype(i32), axis=2)
    g_sel = (g_exp[:, :, None] == jnp.arange(N_EXPERTS, dtype=i32)).astype(i32)
    g_dst = jnp.sum(g_sel * (gstart - boff)[:, None, :], axis=2) + g_row[None, :]
    runs = (g_dst.reshape(-1), jnp.sum(cnt_g, axis=1) // MOE_GROUP)
    gaps = (jnp.concatenate([base + region, ends[-1:]]),
            jnp.concatenate([(padded - region) // MOE_GROUP, n_tiles - n_used.reshape(1)]))
    return runs, gaps, bufpos.reshape(nt, 2, TM), tile_e, n_used.reshape(1), n_tiles


def _run_copies(runs, tile, buf, hbm, sem, *, to_hbm, start):
    row_ref, ng_ref = runs

    def body(c, carry):
        v = buf.at[pl.ds(pl.multiple_of(c * MOE_GROUP, MOE_GROUP) if start else 0, MOE_GROUP)]
        row = pl.multiple_of(row_ref[tile * MOE_GROUPS + c], MOE_GROUP) if start else 0
        h = hbm.at[pl.ds(row, MOE_GROUP)]
        cp = pltpu.make_async_copy(v, h, sem) if to_hbm else pltpu.make_async_copy(h, v, sem)
        if start:
            cp.start()
        else:
            cp.wait()
        return carry
    lax.fori_loop(0, ng_ref[tile], body, 0)


def _moe_dispatch_kernel(row_ref, ng_ref, gap0_ref, gapn_ref, h_ref, bp_ref, xs_hbm, buf, zbuf, sem, zsem):
    runs = (row_ref, ng_ref)
    i = pl.program_id(0)
    n = pl.num_programs(0)
    slot = i % 2
    bp = bp_ref[...]
    j = lax.broadcasted_iota(i32, (MOE_BUF, TM), 0)
    sel = jnp.where((j == bp[0:1]) | (j == bp[1:2]), 1.0, 0.0).astype(bf16)
    buf[slot] = jnp.dot(sel, h_ref[...], preferred_element_type=f32)
    _run_copies(runs, i, buf.at[slot], xs_hbm, sem.at[slot], to_hbm=True, start=True)

    @pl.when(i > 0)
    def _():
        _run_copies(runs, i - 1, buf.at[1 - slot], xs_hbm, sem.at[1 - slot], to_hbm=True, start=False)

    @pl.when(i == n - 1)
    def _():
        _run_copies(runs, i, buf.at[slot], xs_hbm, sem.at[slot], to_hbm=True, start=False)
        zbuf[...] = jnp.zeros(zbuf.shape, f32)
        for start in (True, False):
            for e in range(N_EXPERTS + 1):
                rows = MOE_GROUP if e < N_EXPERTS else TM

                def body(c, carry, e=e, start=start, rows=rows):
                    dst = xs_hbm.at[pl.ds(pl.multiple_of(gap0_ref[e] + c * rows, MOE_GROUP), rows)]
                    cp = pltpu.make_async_copy(zbuf.at[pl.ds(0, rows)], dst, zsem)
                    if start:
                        cp.start()
                    else:
                        cp.wait()
                    return carry
                lax.fori_loop(0, gapn_ref[e], body, 0)


def _moe_dispatch(h, runs, gaps, bufpos, n_tiles):
    t, d = h.shape
    grid_spec = pltpu.PrefetchScalarGridSpec(
        num_scalar_prefetch=4,
        grid=(t // TM,),
        in_specs=[pl.BlockSpec((TM, d), lambda i, *_: (i, 0)),
                  pl.BlockSpec((None, 2, TM), lambda i, *_: (i, 0, 0))],
        out_specs=pl.BlockSpec(memory_space=pl.ANY),
        scratch_shapes=[pltpu.VMEM((2, MOE_BUF, d), f32), pltpu.VMEM((TM, d), f32),
                        pltpu.SemaphoreType.DMA((2,)), pltpu.SemaphoreType.DMA],
    )
    return pl.pallas_call(
        _moe_dispatch_kernel,
        grid_spec=grid_spec,
        out_shape=jax.ShapeDtypeStruct((n_tiles * TM, d), f32),
        compiler_params=_cparams(("arbitrary",)),
        name="moe_dispatch",
    )(*runs, *gaps, h, bufpos)


def _moe_ffn_kernel(te_ref, nu_ref, x_ref, wg_ref, wu_ref, wd_ref, y_ref):
    @pl.when(pl.program_id(0) < nu_ref[0])
    def _():
        x = x_ref[...].astype(bf16)
        y = None
        for c in range(0, wg_ref.shape[1], TM):
            a = _silu(_dot(x, wg_ref[:, c:c + TM])) * _dot(x, wu_ref[:, c:c + TM])
            part = _dot(a, wd_ref[c:c + TM, :])
            y = part if y is None else y + part
        y_ref[...] = y

    @pl.when(pl.program_id(0) >= nu_ref[0])
    def _():
        y_ref[...] = jnp.zeros(y_ref.shape, f32)


def _moe_ffn(xs, tile_e, n_used, wg, wu, wd, layer):
    p, d = xs.shape
    ff = wg.shape[3]
    row = lambda i, te, nu: (jnp.minimum(i, nu[0] - 1), 0)
    grid_spec = pltpu.PrefetchScalarGridSpec(
        num_scalar_prefetch=2,
        grid=(p // TM,),
        in_specs=[pl.BlockSpec((TM, d), row),
                  pl.BlockSpec((None, None, d, ff), lambda i, te, nu: (layer, te[i], 0, 0)),
                  pl.BlockSpec((None, None, d, ff), lambda i, te, nu: (layer, te[i], 0, 0)),
                  pl.BlockSpec((None, None, ff, d), lambda i, te, nu: (layer, te[i], 0, 0))],
        out_specs=pl.BlockSpec((TM, d), lambda i, te, nu: (i, 0)),
    )
    return pl.pallas_call(
        _moe_ffn_kernel,
        grid_spec=grid_spec,
        out_shape=jax.ShapeDtypeStruct((p, d), f32),
        compiler_params=_cparams(("arbitrary",)),
        name="moe_ffn",
    )(tile_e, n_used, xs, wg, wu, wd)


def _moe_combine_kernel(row_ref, ng_ref, y_hbm, x_ref, bp_ref, wt_ref, mod_ref, *rest, final):
    if final:
        fg_ref, o_ref, ybuf, sem = rest
    else:
        o_ref, ybuf, sem = rest
    runs = (row_ref, ng_ref)
    i = pl.program_id(0)
    n = pl.num_programs(0)
    slot = i % 2

    @pl.when(i == 0)
    def _():
        ybuf[...] = jnp.zeros(ybuf.shape, f32)
        _run_copies(runs, 0, ybuf.at[0], y_hbm, sem.at[0], to_hbm=False, start=True)

    @pl.when(i + 1 < n)
    def _():
        _run_copies(runs, i + 1, ybuf.at[1 - slot], y_hbm, sem.at[1 - slot], to_hbm=False, start=True)

    _run_copies(runs, i, ybuf.at[slot], y_hbm, sem.at[slot], to_hbm=False, start=False)
    bp = bp_ref[...]
    w = wt_ref[...]
    j = lax.broadcasted_iota(i32, (MOE_BUF, TM), 0)
    sel_t = jnp.where(j == bp[0:1], w[0:1], 0.0) + jnp.where(j == bp[1:2], w[1:2], 0.0)
    out = x_ref[...] + mod_ref[5:6] * _dot_tn(sel_t, ybuf[slot])
    if final:
        out = _rms_rows(out) * fg_ref[...]
    o_ref[...] = out


def _moe_combine(y, runs, bufpos, x, wts, modl, tiles_per_batch, lat_only, final_g=None):
    t, d = x.shape
    n = t // TM
    off = 1 if lat_only else 0
    final = final_g is not None
    in_specs = [pl.BlockSpec(memory_space=pl.ANY),
                pl.BlockSpec((TM, d), lambda i, *_: (i, 0)),
                pl.BlockSpec((None, 2, TM), lambda i, *_: (i, 0, 0)),
                pl.BlockSpec((2, TM), lambda i, *_: (0, i)),
                pl.BlockSpec((None, None, 6, d),
                             lambda i, *_: (i // tiles_per_batch, jnp.minimum(i % tiles_per_batch + off, 1), 0, 0))]
    args = [y, x, bufpos, wts, modl]
    if final:
        in_specs.append(pl.BlockSpec((1, d), lambda i, *_: (0, 0)))
        args.append(final_g)
    grid_spec = pltpu.PrefetchScalarGridSpec(
        num_scalar_prefetch=2,
        grid=(n,),
        in_specs=in_specs,
        out_specs=pl.BlockSpec((TM, d), lambda i, *_: (i, 0)),
        scratch_shapes=[pltpu.VMEM((2, MOE_BUF, d), f32), pltpu.SemaphoreType.DMA((2,))],
    )
    return pl.pallas_call(
        functools.partial(_moe_combine_kernel, final=final),
        grid_spec=grid_spec,
        out_shape=jax.ShapeDtypeStruct((t, d), f32),
        compiler_params=_cparams(("arbitrary",)),
        name="moe_combine_final" if final else "moe_combine",
    )(*runs, *args)


def _moe(h, x, idx, wts, modl, experts, layer, tiles_per_batch, lat_only, final_g=None):
    runs, gaps, bufpos, tile_e, n_used, n_tiles = _dispatch_plan(idx)
    xs = _moe_dispatch(h, runs, gaps, bufpos, n_tiles)
    y = _moe_ffn(xs, tile_e, n_used, *experts, layer)
    return _moe_combine(y, runs, bufpos, x, wts, modl, tiles_per_batch, lat_only, final_g)


def _cd_in_kernel(x_ref, mod_ref, g_ref, w_ref, ag_ref, qkv_ref, dg_ref, ab_ref):
    mod = mod_ref[...]
    h = _rms_rows(x_ref[...]) * g_ref[...] * (1.0 + mod[1:2]) + mod[0:1]
    acc = _dot(h, w_ref[...])
    c1 = 2 * CV_CH
    c2 = c1 + DN_QKV
    c3 = c2 + DN_QK
    ag_ref[...] = acc[:, 0:c1]
    qkv_ref[...] = acc[:, c1:c2]
    dg_ref[...] = acc[:, c2:c3]
    ab_ref[...] = acc[:, c3:c3 + LANES]


def _cd_in_proj(x, modl, g, w):
    b, l, d = x.shape
    nt = l // TM
    n = w.shape[1]
    full = lambda shape: pl.BlockSpec(shape, lambda bi, i: (0,) * len(shape))
    widths = (2 * CV_CH, DN_QKV, DN_QK, LANES)
    return pl.pallas_call(
        _cd_in_kernel,
        grid=(b, nt),
        in_specs=[pl.BlockSpec((None, TM, d), lambda bi, i: (bi, i, 0)),
                  pl.BlockSpec((None, None, 6, d), _mod_index),
                  full((1, d)), full((d, n))],
        out_specs=[pl.BlockSpec((None, TM, w_), lambda bi, i: (bi, i, 0)) for w_ in widths],
        out_shape=[jax.ShapeDtypeStruct((b, l, w_), f32) for w_ in widths],
        compiler_params=_cparams(("arbitrary", "arbitrary")),
        name="cd_in_proj",
    )(x, modl, g, w)


CV_HALO = 16
CV_ROWS = 32


def _conformer_kernel(prev_ref, cur_ref, next_ref, w_ref, b_ref, lng_ref, lnb_ref, o_ref, ext_ref, sh_ref):
    i = pl.program_id(1)
    last = pl.num_programs(1) - 1

    def glu(z):
        return z[:, 0:CV_CH] * _sigmoid(z[:, CV_CH:2 * CV_CH])

    ext_ref[0:CV_HALO, :] = jnp.where(i > 0, glu(prev_ref[...]), 0.0)
    ext_ref[CV_HALO:CV_HALO + TM, :] = glu(cur_ref[...])
    ext_ref[CV_HALO + TM:2 * CV_HALO + TM, :] = jnp.where(i < last, glu(next_ref[...]), 0.0)
    n_sh = sh_ref.shape[1]
    for r in range(1, SUBLANES):
        sh_ref[r] = ext_ref[pl.ds(r, n_sh), :]
    base = CV_HALO - CV_WIDTH // 2
    groups = CV_ROWS // SUBLANES
    for rb in range(TM // CV_ROWS):
        acc = jnp.zeros((groups, SUBLANES, CV_CH), f32)
        for j in range(CV_WIDTH):
            a, r = divmod(base + j, SUBLANES)
            start = rb * CV_ROWS + a * SUBLANES
            x = ext_ref[pl.ds(start, CV_ROWS), :] if r == 0 else sh_ref[r, pl.ds(start, CV_ROWS), :]
            acc = acc + x.reshape(groups, SUBLANES, CV_CH) * w_ref[j][None]
        y = _layer_norm_rows(acc.reshape(CV_ROWS, CV_CH) + b_ref[...], lng_ref[...], lnb_ref[...])
        o_ref[rb * CV_ROWS:(rb + 1) * CV_ROWS, :] = _silu(y).astype(bf16)


def _conformer(ag, w, bias, lng, lnb):
    b, l, c2 = ag.shape
    nt = l // TM - 1
    hb = TM // CV_HALO
    n_halo = l // CV_HALO
    full = lambda shape: pl.BlockSpec(shape, lambda bi, i: (0,) * len(shape))
    return pl.pallas_call(
        _conformer_kernel,
        grid=(b, nt),
        in_specs=[pl.BlockSpec((None, CV_HALO, c2), lambda bi, i: (bi, (i + 1) * hb - 1, 0)),
                  pl.BlockSpec((None, TM, c2), lambda bi, i: (bi, i + 1, 0)),
                  pl.BlockSpec((None, CV_HALO, c2), lambda bi, i: (bi, jnp.minimum((i + 2) * hb, n_halo - 1), 0)),
                  full((CV_WIDTH, SUBLANES, CV_CH)), full((1, CV_CH)), full((1, CV_CH)), full((1, CV_CH))],
        out_specs=pl.BlockSpec((None, TM, CV_CH), lambda bi, i: (bi, i, 0)),
        out_shape=jax.ShapeDtypeStruct((b, nt * TM, CV_CH), bf16),
        scratch_shapes=[pltpu.VMEM((TM + 2 * CV_HALO, CV_CH), f32),
                        pltpu.VMEM((SUBLANES, TM + 2 * CV_HALO - SUBLANES, CV_CH), f32)],
        compiler_params=_cparams(("arbitrary", "arbitrary")),
        name="conformer",
    )(ag, ag, ag, w, bias, lng, lnb)


DN_HALO = 8


def _delta_feat_kernel(prev_ref, cur_ref, next_ref, ab_ref, w_ref, alog_ref, dtb_ref,
                       q_ref, k_ref, v_ref, g_ref, ext_ref):
    i = pl.program_id(1)
    last = pl.num_programs(1) - 1
    ext_ref[0:DN_HALO, :] = jnp.where(i > 1, prev_ref[...], 0.0)
    ext_ref[DN_HALO:DN_HALO + TM, :] = cur_ref[...]
    ext_ref[DN_HALO + TM:2 * DN_HALO + TM, :] = jnp.where((i > 0) & (i < last), next_ref[...], 0.0)
    w = w_ref[...]
    base = DN_HALO - DN_CONV // 2
    outs = (q_ref, k_ref, v_ref)
    for c in range(DN_QKV // LANES):
        cs = slice(c * LANES, (c + 1) * LANES)
        acc = jnp.zeros((TM, LANES), f32)
        for j in range(DN_CONV):
            acc = acc + ext_ref[pl.ds(base + j, TM), cs] * w[j:j + 1, cs]
        y = _silu(acc)
        part, h = divmod(c, DN_HEADS)
        if part < 2:
            y = y * lax.rsqrt(jnp.sum(y * y, axis=-1, keepdims=True) + EPS)
            if part == 0:
                y = y * (DN_DK ** -0.5)
        outs[part][:, h * LANES:(h + 1) * LANES] = y

    ab = ab_ref[...]
    g = -jnp.exp(alog_ref[...]) * (jnp.maximum(ab + dtb_ref[...], 0.0)
                                   + jnp.log(1.0 + jnp.exp(-jnp.abs(ab + dtb_ref[...]))))
    beta = _sigmoid(ab)
    lane = lax.broadcasted_iota(i32, ab.shape, 1)
    row = lax.broadcasted_iota(i32, (TM, TM), 0)
    col = lax.broadcasted_iota(i32, (TM, TM), 1)
    same = (row // DN_CHUNK) == (col // DN_CHUNK)
    for d in range(N_DIR):
        gd = g if d == 0 else pltpu.roll(g, LANES - DN_HEADS, 1)
        bd = pltpu.roll(beta, LANES - DN_HEADS * (d + 1), 1)
        tri = (same & ((col <= row) if d == 0 else (col >= row))).astype(f32)
        gc = jnp.dot(tri, jnp.where(lane < DN_HEADS, gd, 0.0), precision=lax.Precision.HIGHEST,
                     preferred_element_type=f32)
        g_ref[d] = jnp.where(lane < DN_HEADS, gc, bd)


def _delta_features(qkv, ab, w, alog, dtb):
    b, l, c = qkv.shape
    nt = l // TM
    hb = TM // DN_HALO
    n_halo = l // DN_HALO
    full = lambda shape: pl.BlockSpec(shape, lambda bi, i: (0,) * len(shape))
    return pl.pallas_call(
        _delta_feat_kernel,
        grid=(b, nt),
        in_specs=[pl.BlockSpec((None, DN_HALO, c), lambda bi, i: (bi, jnp.maximum(i * hb - 1, 0), 0)),
                  pl.BlockSpec((None, TM, c), lambda bi, i: (bi, i, 0)),
                  pl.BlockSpec((None, DN_HALO, c), lambda bi, i: (bi, jnp.minimum((i + 1) * hb, n_halo - 1), 0)),
                  pl.BlockSpec((None, TM, LANES), lambda bi, i: (bi, i, 0)),
                  full((DN_CONV, c)), full((1, LANES)), full((1, LANES))],
        out_specs=[pl.BlockSpec((None, TM, DN_QK), lambda bi, i: (bi, i, 0))] * 3
        + [pl.BlockSpec((N_DIR, None, TM, LANES), lambda bi, i: (0, bi, i, 0))],
        out_shape=[jax.ShapeDtypeStruct((b, l, DN_QK), f32)] * 3
        + [jax.ShapeDtypeStruct((N_DIR, b, l, LANES), f32)],
        scratch_shapes=[pltpu.VMEM((TM + 2 * DN_HALO, c), f32)],
        compiler_params=_cparams(("arbitrary", "arbitrary")),
        name="delta_features",
    )(qkv, qkv, qkv, ab, w, alog, dtb)


DN_PRE_CHUNKS = 4
DN_EG_ROWS = 8


def _delta_pre_kernel(q_ref, k_ref, v_ref, g_ref, u_ref, w_ref, qe_ref, kd_ref, qk_ref, eg_ref):
    n = DN_CHUNK
    row = lax.broadcasted_iota(i32, (n, LANES), 0)
    col = lax.broadcasted_iota(i32, (n, LANES), 1)
    eye = row == col
    valid = col < n
    zpad = jnp.zeros((LANES - n, LANES), f32)
    sh = [(s, h) for s in range(DN_PRE_CHUNKS) for h in range(DN_HEADS)]
    rsl = lambda s: slice(s * n, (s + 1) * n)
    csl = lambda h: slice(h * LANES, (h + 1) * LANES)
    q = [q_ref[rsl(s), csl(h)] for s, h in sh]
    k = [k_ref[rsl(s), csl(h)] for s, h in sh]
    v = [v_ref[rsl(s), csl(h)] for s, h in sh]
    k_pad = [jnp.concatenate([ki, zpad], axis=0) for ki in k]
    kk = [_dot_nt(ki, kp) for ki, kp in zip(k, k_pad)]
    qk = [_dot_nt(qi, kp) for qi, kp in zip(q, k_pad)]
    prob = [(i, d) for i in range(len(sh)) for d in range(N_DIR)]
    gc, beta, gl, decay, a = [], [], [], [], []
    for i, d in prob:
        s, h = sh[i]
        gates = g_ref[d, rsl(s), :]
        gci = gates[:, h:h + 1]
        bi = gates[:, DN_HEADS + h:DN_HEADS + h + 1]
        last = n - 1 if d == 0 else 0
        incl = ((col <= row) if d == 0 else (col >= row)) & valid
        strict = ((col < row) if d == 0 else (col > row)) & valid
        gc_row = jnp.sum(jnp.where(eye, gci, 0.0), axis=0, keepdims=True)
        di = jnp.exp(jnp.where(incl, gci - gc_row, -jnp.inf))
        gc.append(gci)
        beta.append(bi)
        gl.append(gates[last:last + 1, h:h + 1])
        decay.append(di)
        a.append(jnp.where(strict, -(bi * kk[i] * di), 0.0))
    y = list(a)
    p = list(a)
    for _ in range(5):
        p = [_dot(pi[:, 0:n], pi) for pi in p]
        yp = [_dot(yi[:, 0:n], pi) for yi, pi in zip(y, p)]
        y = [yi + pi + ypi for yi, pi, ypi in zip(y, p, yp)]
    e_gc = [jnp.exp(g) for g in gc]
    rhs = [jnp.concatenate([v[i] * beta[j], k[i] * (beta[j] * e_gc[j])], axis=1) for j, (i, d) in enumerate(prob)]
    sol = [r + _dot(yi[:, 0:n], r) for yi, r in zip(y, rhs)]
    for j, (i, d) in enumerate(prob):
        s, h = sh[i]
        rs, cs = rsl(s), csl(h)
        u_ref[d, rs, cs] = sol[j][:, 0:LANES]
        w_ref[d, rs, cs] = sol[j][:, LANES:2 * LANES].astype(bf16)
        qe_ref[d, rs, cs] = (q[i] * e_gc[j]).astype(bf16)
        kd_ref[d, rs, cs] = (k[i] * jnp.exp(gl[j] - gc[j])).astype(bf16)
        qk_ref[d, rs, cs] = (qk[i] * decay[j]).astype(bf16)
    for s in range(DN_PRE_CHUNKS):
        for d in range(N_DIR):
            last = s * n + (n - 1 if d == 0 else 0)
            eg_ref[d, s * DN_EG_ROWS:(s + 1) * DN_EG_ROWS, :] = jnp.broadcast_to(
                jnp.exp(g_ref[d, last:last + 1, :]), (DN_EG_ROWS, LANES))


def _delta_pre(q, k, v, gates):
    b, l, c = q.shape
    rows = DN_PRE_CHUNKS * DN_CHUNK
    n = l // rows
    row = pl.BlockSpec((None, rows, c), lambda bi, i: (bi, i, 0))
    out = pl.BlockSpec((N_DIR, None, rows, c), lambda bi, i: (0, bi, i, 0))
    eg_rows = DN_PRE_CHUNKS * DN_EG_ROWS
    return pl.pallas_call(
        _delta_pre_kernel,
        grid=(b, n),
        in_specs=[row, row, row, pl.BlockSpec((N_DIR, None, rows, LANES), lambda bi, i: (0, bi, i, 0))],
        out_specs=[out] * 5 + [pl.BlockSpec((N_DIR, None, eg_rows, LANES), lambda bi, i: (0, bi, i, 0))],
        out_shape=[jax.ShapeDtypeStruct((N_DIR, b, l, c), f32)]
        + [jax.ShapeDtypeStruct((N_DIR, b, l, c), bf16)] * 4
        + [jax.ShapeDtypeStruct((N_DIR, b, n * eg_rows, LANES), f32)],
        compiler_params=_cparams(("arbitrary", "arbitrary")),
        name="delta_pre",
    )(q, k, v, gates)


def _delta_scan_kernel(*refs):
    n_in = 6 * N_DIR
    o_refs = refs[n_in:n_in + N_DIR]
    s_ref = refs[n_in + N_DIR]
    n = DN_CHUNK
    n_sub = TM // DN_CHUNK

    @pl.when(pl.program_id(1) == 0)
    def _():
        s_ref[...] = jnp.zeros(s_ref.shape, f32)

    dh = [(d, h) for d in range(N_DIR) for h in range(DN_HEADS)]
    csl = lambda h: slice(h * LANES, (h + 1) * LANES)
    ins = lambda d: refs[6 * d:6 * d + 6]
    s = [s_ref[d, h] for d, h in dh]
    for c in range(n_sub):
        sub = lambda d: c if d == 0 else n_sub - 1 - c
        rsl = lambda d: slice(sub(d) * n, (sub(d) + 1) * n)
        r = [_dot(jnp.concatenate([ins(d)[1][rsl(d), csl(h)], ins(d)[2][rsl(d), csl(h)]], axis=0), si)
             for (d, h), si in zip(dh, s)]
        v_new = [ins(d)[0][rsl(d), csl(h)] - ri[0:n] for (d, h), ri in zip(dh, r)]
        s_add = [_dot_tn(ins(d)[3][rsl(d), csl(h)], vi) for (d, h), vi in zip(dh, v_new)]
        o_add = [_dot(ins(d)[4][rsl(d), h * LANES:h * LANES + n], vi) for (d, h), vi in zip(dh, v_new)]
        for j, (d, h) in enumerate(dh):
            o_refs[d][rsl(d), csl(h)] = r[j][n:2 * n] + o_add[j]
        s = [s[j] * ins(d)[5][sub(d) * DN_EG_ROWS:sub(d) * DN_EG_ROWS + 1, h:h + 1] + s_add[j]
             for j, (d, h) in enumerate(dh)]
    for j, (d, h) in enumerate(dh):
        s_ref[d, h] = s[j]


def _delta_scan(u, w, qe, kd, qk, eg):
    _, b, l, c = u.shape
    nt = l // TM
    eg_rows = TM // DN_CHUNK * DN_EG_ROWS

    def tile(d, ti):
        if d == 0:
            return ti
        return jnp.where(ti == 0, 0, nt - ti)

    in_specs, args = [], []
    for d in range(N_DIR):
        spec = pl.BlockSpec((None, None, TM, c), lambda bi, ti, d=d: (d, bi, tile(d, ti), 0))
        in_specs += [spec] * 5 + [pl.BlockSpec((None, None, eg_rows, LANES),
                                               lambda bi, ti, d=d: (d, bi, tile(d, ti), 0))]
        args += [u, w, qe, kd, qk, eg]
    return pl.pallas_call(
        _delta_scan_kernel,
        grid=(b, nt),
        in_specs=in_specs,
        out_specs=[pl.BlockSpec((None, TM, c), lambda bi, ti, d=d: (bi, tile(d, ti), 0))
                   for d in range(N_DIR)],
        out_shape=[jax.ShapeDtypeStruct((b, l, c), f32)] * N_DIR,
        scratch_shapes=[pltpu.VMEM((N_DIR, DN_HEADS, DN_DK, LANES), f32)],
        compiler_params=_cparams(("arbitrary", "arbitrary")),
        name="delta_scan",
    )(*args)


def _rope_tables(s, ctx_len):
    rows = s // GRID_W
    row = jnp.repeat(jnp.arange(rows, dtype=i32), GRID_W)
    col = jnp.tile(jnp.arange(GRID_W, dtype=i32), rows)
    freqs = ROPE_THETA ** (-jnp.arange(ROPE_PAIRS, dtype=f32) / ROPE_PAIRS)
    ang = jnp.stack([row, col], axis=-1).astype(f32)[..., None] * freqs
    cos = jnp.cos(ang)
    sin = jnp.sin(ang)
    cos_h = jnp.concatenate([cos[:, 0], cos[:, 0], cos[:, 1], cos[:, 1]], axis=-1)
    sin_h = jnp.concatenate([-sin[:, 0], sin[:, 0], -sin[:, 1], sin[:, 1]], axis=-1)
    cos_t = jnp.concatenate([jnp.ones((ctx_len, HEAD_DIM), f32), cos_h], axis=0)
    sin_t = jnp.concatenate([jnp.zeros((ctx_len, HEAD_DIM), f32), sin_h], axis=0)
    return jnp.tile(cos_t, (1, 2)), jnp.tile(sin_t, (1, 2))


def _pad_lanes(v, n=LANES):
    v = v.reshape(1, -1)
    return jnp.pad(v, ((0, 0), (0, n - v.shape[1])))


def kernel(x, c, ctx, c_ctx, mod_w, mod_b, norm1_g, norm2_g, ab_w_in, ab_q_norm, ab_k_norm, gm_ln_g, gm_ln_b, gm_w_s, gm_b_s, ab_w_out, cd_w_in, cv_dw_w, cv_dw_b, cv_ln_g, cv_ln_b, dn_conv_w, dn_a_log, dn_dt_bias, dn_o_norm, cd_w_out, router_w, router_b, moe_w_gate, moe_w_up, moe_w_down, final_norm_g):
    b, s, d = x.shape
    ctx_len = ctx.shape[1]
    assert ctx_len == TM and s % (4 * TM) == 0 and mod_w.shape[0] == 2
    l = ctx_len + s
    nt = l // TM

    n_rows = -(-(b + 1) // 8) * 8
    cc = jnp.concatenate([c, c_ctx[None, :], jnp.zeros((n_rows - b - 1, d), f32)], axis=0)
    mod = _modulation(cc, mod_w, mod_b)
    mod_lat = mod[:, :b].reshape(2, b, 1, 6, d)
    mod_ctx = jnp.broadcast_to(mod[:, b].reshape(2, 1, 1, 6, d), (2, b, 1, 6, d))
    modl = jnp.concatenate([mod_ctx, mod_lat], axis=2)

    rw32 =jnp.pad(router_w, ((0, 0), (0, LANES - N_EXPERTS)))
    rw_hi = rw32.astype(bf16)
    rw = jnp.concatenate([rw_hi, (rw32 - rw_hi.astype(f32)).astype(bf16)], axis=1)
    rb = router_b.reshape(N_EXPERTS, 1)
    experts = (moe_w_gate, moe_w_up, moe_w_down)

    cos_t, sin_t = _rope_tables(s, ctx_len)
    lane = jnp.arange(2 * LANES)
    seg = ((lane[:, None] // HEAD_DIM) == (lane[None, :] // HEAD_DIM)).astype(bf16) * (1.0 / HEAD_DIM)
    q, kd, vt, sg = _ab_in_proj(
        ctx, x, modl[0], norm1_g[0:1], ab_w_in[0].astype(bf16), seg,
        jnp.tile(ab_q_norm[0], 2)[None, :], jnp.tile(ab_k_norm[0], 2)[None, :], cos_t, sin_t,
        gm_ln_g[0:1], gm_ln_b[0:1], gm_w_s[0].astype(bf16),
        jnp.broadcast_to(gm_b_s[0][:, :, None], (GM_GROUPS, GM_CHUNK, LANES)))
    att = _attention(q, kd, vt)
    x1, h1, idx, wts = _out_proj(att, sg, (ctx, x), modl[0], norm2_g[0:1], ab_w_out[0].astype(bf16), rw, rb,
                                 delta=False)
    xs = _moe(h1, x1, idx, wts, modl[0], experts, 0, nt, False).reshape(b, l, d)

    c1 = 2 * CV_CH
    c2 = c1 + DN_QKV
    c3 = c2 + 2 * N_DIR * DN_HEADS
    w_in = cd_w_in[0]
    w_cd = jnp.concatenate([w_in[:, 0:c2], w_in[:, c3:], w_in[:, c2:c3],
                            jnp.zeros((d, LANES - (c3 - c2)), f32)], axis=1).astype(bf16)
    ag, qkv, dg, ab = _cd_in_proj(xs, modl[1], norm1_g[1:2], w_cd)
    conv_w = jnp.broadcast_to(cv_dw_w[0][:, None, :], (CV_WIDTH, SUBLANES, CV_CH))
    conv = _conformer(ag, conv_w, cv_dw_b[0:1], cv_ln_g[0:1], cv_ln_b[0:1])
    qn, kn, vv, gates = _delta_features(qkv, ab, dn_conv_w[0], _pad_lanes(dn_a_log[0]), _pad_lanes(dn_dt_bias[0]))
    o_f, o_b = _delta_scan(*_delta_pre(qn, kn, vv, gates))
    x2, h2, idx, wts = _out_proj(conv, (o_f, o_b, dg, dn_o_norm[0:1]), xs, modl[1], norm2_g[1:2],
                                 cd_w_out[0].astype(bf16), rw, rb, delta=True)
    out = _moe(h2, x2, idx, wts, modl[1], experts, 1, nt - 1, True, final_norm_g[None, :])
    return out.reshape(b, s, d)
```

```python
import functools
import math

import jax
import jax.numpy as jnp
from jax import lax
from jax.experimental import pallas as pl
from jax.experimental.pallas import tpu as pltpu

f32 = jnp.float32
bf16 = jnp.bfloat16
i32 = jnp.int32

GRID_W = 64
EPS = 1e-6
ATT_HEADS = 8
ATT_KV_HEADS = 2
HEAD_DIM = 64
ROPE_PAIRS = HEAD_DIM // 4
ROPE_THETA = 10000.0
A_Q = ATT_HEADS * HEAD_DIM
A_KV = ATT_KV_HEADS * HEAD_DIM
GM_GROUPS = 4
GM_CHUNK = 128
GM_CH = 512
CV_CH = 512
CV_WIDTH = 31
DN_HEADS = 4
DN_DK = 128
DN_CONV = 5
DN_CHUNK = 64
N_DIR = 2
DN_QK = DN_HEADS * DN_DK
DN_QKV = 3 * DN_QK
N_EXPERTS = 16
N_GROUPS = 4
EXPERT_FF = 512

LANES = 128
SUBLANES = 8
TM = 256
V_ROWS = HEAD_DIM + 16
VMEM_LIMIT = 48 * 1024 * 1024


def _cparams(sem):
    return pltpu.CompilerParams(dimension_semantics=sem, vmem_limit_bytes=VMEM_LIMIT)


def _sigmoid(x):
    return 1.0 / (1.0 + jnp.exp(-x))


def _silu(x):
    return x * _sigmoid(x)


def _gelu(x):
    return x * (0.5 * (1.0 + jnp.tanh(math.sqrt(2.0 / math.pi) * (x + 0.044715 * (x * x * x)))))


def _rms_rows(x):
    return x * lax.rsqrt(jnp.mean(x * x, axis=-1, keepdims=True) + EPS)


def _layer_norm_rows(x, g, b):
    mu = jnp.mean(x, axis=-1, keepdims=True)
    xc = x - mu
    var = jnp.mean(xc * xc, axis=-1, keepdims=True)
    return xc * lax.rsqrt(var + EPS) * g + b


def _dot(a, b):
    return jnp.dot(a.astype(bf16), b.astype(bf16), preferred_element_type=f32)


def _dot_nt(a, b):
    return lax.dot_general(a.astype(bf16), b.astype(bf16), (((1,), (1,)), ((), ())), preferred_element_type=f32)


def _dot_tn(a, b):
    return lax.dot_general(a.astype(bf16), b.astype(bf16), (((0,), (0,)), ((), ())), preferred_element_type=f32)


def _mod_kernel(c_ref, w_ref, b_ref, o_ref):
    c = c_ref[...]
    o_ref[...] = jnp.dot(_silu(c), w_ref[...], precision=lax.Precision.HIGHEST,
                         preferred_element_type=f32) + b_ref[...]


def _modulation(cc, mod_w, mod_b):
    depth, d, n = mod_w.shape
    tn = 1024
    rows = cc.shape[0]
    return pl.pallas_call(
        _mod_kernel,
        grid=(depth, n // tn),
        in_specs=[pl.BlockSpec((rows, d), lambda l, j: (0, 0)),
                  pl.BlockSpec((None, d, tn), lambda l, j: (l, 0, j)),
                  pl.BlockSpec((None, 1, tn), lambda l, j: (l, 0, j))],
        out_specs=pl.BlockSpec((None, rows, tn), lambda l, j: (l, 0, j)),
        out_shape=jax.ShapeDtypeStruct((depth, rows, n), f32),
        compiler_params=_cparams(("arbitrary", "arbitrary")),
        name="modulation",
    )(cc, mod_w, mod_b.reshape(depth, 1, n))


def _mod_index(b, i):
    return (b, jnp.minimum(i, 1), 0, 0)


def _head_norm_rope(x, seg_mean, gain, cos, sin, scale):
    xx = x * x
    outs = []
    lane = lax.broadcasted_iota(i32, (x.shape[0], LANES), 1)
    first = (lane % 32) < 16
    for c0 in range(0, x.shape[1], 2 * LANES):
        w = min(2 * LANES, x.shape[1] - c0)
        ms = _dot(xx[:, c0:c0 + w], seg_mean[0:w, 0:w])
        xn = x[:, c0:c0 + w] * lax.rsqrt(ms + EPS)
        for c1 in range(0, w, LANES):
            y = xn[:, c1:c1 + LANES] * gain
            partner = jnp.where(first, pltpu.roll(y, LANES - 16, 1), pltpu.roll(y, 16, 1))
            outs.append((y * cos + partner * sin) * scale)
    return outs


def _tile_rows(ctx_ref, x_ref):
    return jnp.where(pl.program_id(1) == 0, ctx_ref[...], x_ref[...])


def _ctx_lat_specs(d):
    return [pl.BlockSpec((None, TM, d), lambda bi, i: (bi, 0, 0)),
            pl.BlockSpec((None, TM, d), lambda bi, i: (bi, jnp.maximum(i - 1, 0), 0))]


def _ab_in_kernel(ctx_ref, x_ref, mod_ref, g_ref, w_ref, seg_ref, qg_ref, kg_ref, cos_ref, sin_ref,
                  lng_ref, lnb_ref, ws_ref, bs_ref, q_ref, kd_ref, vt_ref, s_ref):
    mod = mod_ref[...]
    h = _rms_rows(_tile_rows(ctx_ref, x_ref)) * g_ref[...] * (1.0 + mod[1:2]) + mod[0:1]
    acc = _dot(h, w_ref[...])
    cos = cos_ref[...]
    sin = sin_ref[...]
    seg = seg_ref[...]

    q = _head_norm_rope(acc[:, 0:A_Q], seg, qg_ref[...], cos, sin, HEAD_DIM ** -0.5 * math.log2(math.e))
    for c, qc in enumerate(q):
        q_ref[:, c * LANES:(c + 1) * LANES] = qc.astype(bf16)

    (k,) = _head_norm_rope(acc[:, A_Q:A_Q + A_KV], seg, kg_ref[...], cos, sin, 1.0)
    lane = lax.broadcasted_iota(i32, k.shape, 1)
    swapped = pltpu.roll(k, HEAD_DIM, 1)
    kd_ref[0] = jnp.where(lane < HEAD_DIM, k, swapped).astype(bf16)
    kd_ref[1] = jnp.where(lane < HEAD_DIM, swapped, k).astype(bf16)

    vt = acc[:, A_Q + A_KV:A_Q + 2 * A_KV].T
    ones = jnp.ones((V_ROWS - HEAD_DIM, vt.shape[1]), bf16)
    for j in range(ATT_KV_HEADS):
        vt_ref[j, 0:HEAD_DIM, :] = vt[j * HEAD_DIM:(j + 1) * HEAD_DIM].astype(bf16)
        vt_ref[j, HEAD_DIM:V_ROWS, :] = ones

    c0 = A_Q + 2 * A_KV
    gu = _gelu(acc[:, c0:c0 + GM_CH])
    ln = _layer_norm_rows(_gelu(acc[:, c0 + GM_CH:c0 + 2 * GM_CH]), lng_ref[...], lnb_ref[...])
    for n in range(acc.shape[0] // GM_CHUNK):
        r = slice(n * GM_CHUNK, (n + 1) * GM_CHUNK)
        for g in range(GM_GROUPS):
            cs = slice(g * LANES, (g + 1) * LANES)
            mixed = _dot(ws_ref[g], ln[r, cs]) + bs_ref[g]
            s_ref[r, cs] = (gu[r, cs] * mixed).astype(bf16)


def _ab_in_proj(ctx, x, modl, g, w, seg, qg, kg, cos, sin, lng, lnb, ws, bs):
    b, s, d = x.shape
    l = s + ctx.shape[1]
    nt = l // TM
    n = w.shape[1]
    full = lambda shape: pl.BlockSpec(shape, lambda bi, i: (0,) * len(shape))
    return pl.pallas_call(
        _ab_in_kernel,
        grid=(b, nt),
        in_specs=_ctx_lat_specs(d) + [
                  pl.BlockSpec((None, None, 6, d), _mod_index),
                  full((1, d)), full((d, n)), full((2 * LANES, 2 * LANES)),
                  full((1, LANES)), full((1, LANES)),
                  pl.BlockSpec((TM, LANES), lambda bi, i: (i, 0)),
                  pl.BlockSpec((TM, LANES), lambda bi, i: (i, 0)),
                  full((1, GM_CH)), full((1, GM_CH)),
                  full((GM_GROUPS, GM_CHUNK, GM_CHUNK)), full((GM_GROUPS, GM_CHUNK, LANES))],
        out_specs=[pl.BlockSpec((None, TM, A_Q), lambda bi, i: (bi, i, 0)),
                   pl.BlockSpec((None, ATT_KV_HEADS, TM, LANES), lambda bi, i: (bi, 0, i, 0)),
                   pl.BlockSpec((None, ATT_KV_HEADS, None, V_ROWS, TM), lambda bi, i: (bi, 0, i, 0, 0)),
                   pl.BlockSpec((None, TM, GM_CH), lambda bi, i: (bi, i, 0))],
        out_shape=[jax.ShapeDtypeStruct((b, l, A_Q), bf16),
                   jax.ShapeDtypeStruct((b, ATT_KV_HEADS, l, LANES), bf16),
                   jax.ShapeDtypeStruct((b, ATT_KV_HEADS, nt, V_ROWS, TM), bf16),
                   jax.ShapeDtypeStruct((b, l, GM_CH), bf16)],
        compiler_params=_cparams(("arbitrary", "arbitrary")),
        name="ab_in_proj",
    )(ctx, x, modl, g, w, seg, qg, kg, cos, sin, lng, lnb, ws, bs)


def _attn_kernel(q_ref, k_ref, vt_ref, o_ref, acc_ref, m_ref, st_ref, st2_ref, *, n_pairs):
    i = pl.program_id(2)
    q = q_ref[...]
    lane = lax.broadcasted_iota(i32, (TM, LANES), 1)
    zero = jnp.zeros((TM, LANES), bf16)
    qs = []
    for p in range(2):
        qp = q[:, p * LANES:(p + 1) * LANES]
        qs.append(jnp.where(lane < HEAD_DIM, qp, zero))
        qs.append(jnp.where(lane < HEAD_DIM, zero, qp))
    heads = range(len(qs))
    hsl = lambda h: slice(h * TM, (h + 1) * TM)
    m_ref[...] = jnp.full(m_ref.shape, -jnp.inf, f32)
    acc_ref[...] = jnp.zeros(acc_ref.shape, f32)

    def scores(kb, h):
        return _dot_nt(kb, qs[h])

    def update(st, vb, h):
        m_old = m_ref[:, hsl(h)]
        m_new = jnp.maximum(m_old, jnp.max(st, axis=0, keepdims=True))
        p = jnp.exp2(st - m_new).astype(bf16)
        acc_ref[:, hsl(h)] = (acc_ref[:, hsl(h)] * jnp.exp2(m_old - m_new)
                              + jnp.dot(vb, p, preferred_element_type=f32))
        m_ref[:, hsl(h)] = m_new

    def latent_keys(t):
        return k_ref[pl.ds(pl.multiple_of(TM + t * (2 * TM), TM), 2 * TM), :]

    def latent_values(t):
        return jnp.concatenate([vt_ref[1 + 2 * t], vt_ref[2 + 2 * t]], axis=1)

    kb0 = k_ref[0:TM, :]

    @pl.when(i == 0)
    def _():
        for h in heads:
            update(scores(kb0, h), vt_ref[0], h)

    @pl.when(i > 0)
    def _():
        def step(t, cur, nxt):
            kb_next = latent_keys(t + 1)
            vb = latent_values(t)
            for h in heads:
                nxt[:, hsl(h)] = scores(kb_next, h)
                update(cur[:, hsl(h)], vb, h)

        kb = latent_keys(0)
        st_c = [scores(kb0, h) for h in heads]
        for h in heads:
            st_ref[:, hsl(h)] = scores(kb, h)
            update(st_c[h], vt_ref[0], h)

        for j in range((n_pairs - 2) // 2):
            step(2 * j, st_ref, st2_ref)
            step(2 * j + 1, st2_ref, st_ref)
        step(n_pairs - 2, st_ref, st2_ref)
        vb = latent_values(n_pairs - 1)
        for h in heads:
            update(st2_ref[:, hsl(h)], vb, h)

    acc = acc_ref[...]
    o = acc[0:HEAD_DIM] / acc[HEAD_DIM:HEAD_DIM + 1]
    for p in range(2):
        pair = jnp.concatenate([o[:, (2 * p) * TM:(2 * p + 1) * TM],
                                o[:, (2 * p + 1) * TM:(2 * p + 2) * TM]], axis=0)
        o_ref[:, p * LANES:(p + 1) * LANES] = pair.T.astype(bf16)


def _attention(q, kd, vt):
    b, l, _ = q.shape
    nt = l // TM
    group_w = A_Q // ATT_KV_HEADS
    assert (l - TM) % (4 * TM) == 0
    return pl.pallas_call(
        functools.partial(_attn_kernel, n_pairs=(l - TM) // (2 * TM)),
        grid=(b, ATT_KV_HEADS, nt),
        in_specs=[pl.BlockSpec((None, TM, group_w), lambda bi, j, i: (bi, i, j)),
                  pl.BlockSpec((None, None, l, LANES), lambda bi, j, i: (bi, j, 0, 0)),
                  pl.BlockSpec((None, None, nt, V_ROWS, TM), lambda bi, j, i: (bi, j, 0, 0, 0))],
        out_specs=pl.BlockSpec((None, TM, group_w), lambda bi, j, i: (bi, i, j)),
        out_shape=jax.ShapeDtypeStruct((b, l, A_Q), bf16),
        scratch_shapes=[pltpu.VMEM((V_ROWS, 4 * TM), f32), pltpu.VMEM((1, 4 * TM), f32),
                        pltpu.VMEM((2 * TM, 4 * TM), f32), pltpu.VMEM((2 * TM, 4 * TM), f32)],
        compiler_params=_cparams(("arbitrary", "arbitrary", "arbitrary")),
        name="attention",
    )(q, kd, vt)


def _route_rows(scores, sel):
    s = [sel[e:e + 1] for e in range(N_EXPERTS)]
    sc = [scores[e:e + 1] for e in range(N_EXPERTS)]
    per = N_EXPERTS // N_GROUPS
    gs = []
    for g in range(N_GROUPS):
        a, b, c, d = s[per * g:per * g + per]
        hi1, lo1 = jnp.maximum(a, b), jnp.minimum(a, b)
        hi2, lo2 = jnp.maximum(c, d), jnp.minimum(c, d)
        gs.append(jnp.maximum(hi1, hi2) + jnp.maximum(jnp.minimum(hi1, hi2), jnp.maximum(lo1, lo2)))
    best = jnp.zeros(gs[0].shape, i32)
    best_v = gs[0]
    for g in range(1, N_GROUPS):
        better = gs[g] > best_v
        best = jnp.where(better, g, best)
        best_v = jnp.where(better, gs[g], best_v)
    v, w = [], []
    for j in range(per):
        vj, wj = s[j], sc[j]
        for g in range(1, N_GROUPS):
            vj = jnp.where(best == g, s[per * g + j], vj)
            wj = jnp.where(best == g, sc[per * g + j], wj)
        v.append(vj)
        w.append(wj)
    i1 = jnp.zeros(best.shape, i32)
    m1, w1 = v[0], w[0]
    for j in range(1, per):
        better = v[j] > m1
        i1 = jnp.where(better, j, i1)
        m1 = jnp.where(better, v[j], m1)
        w1 = jnp.where(better, w[j], w1)
    i2 = jnp.zeros(best.shape, i32)
    m2 = jnp.full(m1.shape, -jnp.inf, f32)
    w2 = jnp.zeros(m1.shape, f32)
    for j in range(per):
        cand = jnp.where(i1 == j, -jnp.inf, v[j])
        better = cand > m2
        i2 = jnp.where(better, j, i2)
        m2 = jnp.where(better, cand, m2)
        w2 = jnp.where(better, w[j], w2)
    tot = w1 + w2
    return best * per + i1, best * per + i2, w1 / tot, w2 / tot


def _out_kernel(*refs, delta):
    if delta:
        (a1_ref, of_ref, ob_ref, dg_ref, on_ref, x_ref, mod_ref, g_ref, w_ref, rw_ref, rb_ref,
         xo_ref, h_ref, idx_ref, wt_ref) = refs
        o = of_ref[...] + ob_ref[...]
        dg = dg_ref[...]
        parts = []
        for h in range(DN_HEADS):
            cs = slice(h * LANES, (h + 1) * LANES)
            parts.append(_rms_rows(o[:, cs]) * on_ref[...] * _silu(dg[:, cs]))
        a2 = jnp.concatenate(parts, axis=1)
        x = x_ref[...]
    else:
        (a1_ref, a2_ref, ctx_ref, x_ref, mod_ref, g_ref, w_ref, rw_ref, rb_ref,
         xo_ref, h_ref, idx_ref, wt_ref) = refs
        a2 = a2_ref[...]
        x = _tile_rows(ctx_ref, x_ref)
    half = a1_ref.shape[-1]
    y = _dot(a1_ref[...], w_ref[0:half, :]) + _dot(a2, w_ref[half:2 * half, :])
    mod = mod_ref[...]
    xn = x + mod[2:3] * y
    xo_ref[...] = xn
    h2 = _rms_rows(xn) * g_ref[...] * (1.0 + mod[4:5]) + mod[3:4]
    h_ref[...] = h2.astype(bf16)
    h_hi = h2.astype(bf16)
    h_lo = (h2 - h_hi.astype(f32)).astype(bf16)
    part = jnp.dot(h_hi, rw_ref[...], preferred_element_type=f32)
    logits = (part[:, 0:LANES] + part[:, LANES:2 * LANES]
              + jnp.dot(h_lo, rw_ref[:, 0:LANES], preferred_element_type=f32))
    scores = _sigmoid(logits.T[0:N_EXPERTS])
    e1, e2, w1, w2 = _route_rows(scores, scores + rb_ref[...])
    idx_ref[0:1, :] = e1
    idx_ref[1:2, :] = e2
    wt_ref[0:1, :] = w1
    wt_ref[1:2, :] = w2


def _out_proj(a1, a2s, x, modl, g, w, rw, rb, *, delta):
    if delta:
        b, l, d = x.shape
        x_specs, x_args, off = [pl.BlockSpec((None, TM, d), lambda bi, i: (bi, i + 1, 0))], [x], 1
    else:
        b, s, d = x[1].shape
        l = s + x[0].shape[1]
        x_specs, x_args, off = _ctx_lat_specs(d), list(x), 0
    nt = l // TM
    nto = nt - off
    half = a1.shape[-1]
    full = lambda shape: pl.BlockSpec(shape, lambda bi, i: (0,) * len(shape))
    row_l = lambda w_: pl.BlockSpec((None, TM, w_), lambda bi, i: (bi, i + off, 0))
    if delta:
        a2_specs = [row_l(half), row_l(half), row_l(half), full((1, LANES))]
        a2_args = list(a2s)
        a1_spec = pl.BlockSpec((None, TM, half), lambda bi, i: (bi, i, 0))
    else:
        a2_specs = [row_l(half)]
        a2_args = [a2s]
        a1_spec = row_l(half)
    t_out = b * nto * TM
    flat = lambda bi, i: (bi * nto + i, 0)
    return pl.pallas_call(
        functools.partial(_out_kernel, delta=delta),
        grid=(b, nto),
        in_specs=[a1_spec] + a2_specs + x_specs + [
            pl.BlockSpec((None, None, 6, d), lambda bi, i: (bi, jnp.minimum(i + off, 1), 0, 0)),
            full((1, d)), full((2 * half, d)), full((d, 2 * LANES)), full((N_EXPERTS, 1))],
        out_specs=[pl.BlockSpec((TM, d), flat), pl.BlockSpec((TM, d), flat),
                   pl.BlockSpec((2, TM), lambda bi, i: (0, bi * nto + i)),
                   pl.BlockSpec((2, TM), lambda bi, i: (0, bi * nto + i))],
        out_shape=[jax.ShapeDtypeStruct((t_out, d), f32), jax.ShapeDtypeStruct((t_out, d), bf16),
                   jax.ShapeDtypeStruct((2, t_out), i32), jax.ShapeDtypeStruct((2, t_out), f32)],
        compiler_params=_cparams(("arbitrary", "arbitrary")),
        name="out_proj_delta" if delta else "out_proj",
    )(a1, *a2_args, *x_args, modl, g, w, rw, rb)


MOE_GROUP = 8
MOE_BUF = 2 * TM + N_EXPERTS * MOE_GROUP
MOE_GROUPS = MOE_BUF // MOE_GROUP
FFN_TM = 2 * TM


def _dispatch_plan(idx):
    t = idx.shape[1]
    nt = t // TM
    e = idx.reshape(2, nt, TM).transpose(1, 0, 2).reshape(nt, 2 * TM)
    onehot = (e[:, :, None] == jnp.arange(N_EXPERTS, dtype=i32)).astype(i32)
    lrank = jnp.cumsum(onehot, axis=1) - onehot
    cnt = jnp.sum(onehot, axis=1)
    cnt_g = (cnt + MOE_GROUP - 1) // MOE_GROUP * MOE_GROUP
    boff = jnp.cumsum(cnt_g, axis=1) - cnt_g
    bufpos = jnp.sum((boff[:, None, :] + lrank) * onehot, axis=2)
    region = jnp.sum(cnt_g, axis=0)
    padded = (region + FFN_TM - 1) // FFN_TM * FFN_TM
    ends = jnp.cumsum(padded)
    base = ends - padded
    gstart = base[None, :] + jnp.cumsum(cnt_g, axis=0) - cnt_g
    n_tiles = -(-(2 * t + nt * N_EXPERTS * (MOE_GROUP - 1)) // FFN_TM) + N_EXPERTS
    n_used = (ends[-1] // FFN_TM).astype(i32)
    tiles = jnp.arange(n_tiles, dtype=i32)
    tile_e = jnp.sum((tiles[:, None] * FFN_TM >= ends[None, :]).astype(i32), axis=1)
    last_e = jnp.sum(((n_used - 1) * FFN_TM >= ends).astype(i32))
    tile_e = jnp.where(tiles < n_used, tile_e, last_e)
    g_row = jnp.arange(MOE_GROUPS, dtype=i32) * MOE_GROUP
    g_exp = jnp.sum((g_row[None, :, None] >= (boff + cnt_g)[:, None, :]).astype(i32), axis=2)
    g_sel = (g_exp[:, :, None] == jnp.arange(N_EXPERTS, dtype=i32)).astype(i32)
    g_dst = jnp.sum(g_sel * (gstart - boff)[:, None, :], axis=2) + g_row[None, :]
    runs = (g_dst.reshape(-1), jnp.sum(cnt_g, axis=1) // MOE_GROUP)
    gaps = (jnp.concatenate([base + region, ends[-1:]]),
            jnp.concatenate([(padded - region) // MOE_GROUP, n_tiles - n_used.reshape(1)]))
    return runs, gaps, bufpos.reshape(nt, 2, TM), tile_e, n_used.reshape(1), n_tiles


def _run_copies(runs, tile, buf, hbm, sem, *, to_hbm, start):
    row_ref, ng_ref = runs

    def body(c, carry):
        v = buf.at[pl.ds(pl.multiple_of(c * MOE_GROUP, MOE_GROUP) if start else 0, MOE_GROUP)]
        row = pl.multiple_of(row_ref[tile * MOE_GROUPS + c], MOE_GROUP) if start else 0
        h = hbm.at[pl.ds(row, MOE_GROUP)]
        cp = pltpu.make_async_copy(v, h, sem) if to_hbm else pltpu.make_async_copy(h, v, sem)
        if start:
            cp.start()
        else:
            cp.wait()
        return carry
    lax.fori_loop(0, ng_ref[tile], body, 0)


def _moe_dispatch_kernel(row_ref, ng_ref, gap0_ref, gapn_ref, h_ref, bp_ref, xs_hbm, buf, zbuf, sem, zsem):
    runs = (row_ref, ng_ref)
    i = pl.program_id(0)
    n = pl.num_programs(0)
    slot = i % 2
    bp = bp_ref[...]
    j = lax.broadcasted_iota(i32, (MOE_BUF, TM), 0)
    sel = jnp.where((j == bp[0:1]) | (j == bp[1:2]), 1.0, 0.0).astype(bf16)
    buf[slot] = jnp.dot(sel, h_ref[...], preferred_element_type=f32)
    _run_copies(runs, i, buf.at[slot], xs_hbm, sem.at[slot], to_hbm=True, start=True)

    @pl.when(i > 0)
    def _():
        _run_copies(runs, i - 1, buf.at[1 - slot], xs_hbm, sem.at[1 - slot], to_hbm=True, start=False)

    @pl.when(i == n - 1)
    def _():
        _run_copies(runs, i, buf.at[slot], xs_hbm, sem.at[slot], to_hbm=True, start=False)
        zbuf[...] = jnp.zeros(zbuf.shape, f32)
        for start in (True, False):
            for e in range(N_EXPERTS + 1):
                rows = MOE_GROUP if e < N_EXPERTS else FFN_TM

                def body(c, carry, e=e, start=start, rows=rows):
                    dst = xs_hbm.at[pl.ds(pl.multiple_of(gap0_ref[e] + c * rows, MOE_GROUP), rows)]
                    cp = pltpu.make_async_copy(zbuf.at[pl.ds(0, rows)], dst, zsem)
                    if start:
                        cp.start()
                    else:
                        cp.wait()
                    return carry
                lax.fori_loop(0, gapn_ref[e], body, 0)


def _moe_dispatch(h, runs, gaps, bufpos, n_tiles):
    t, d = h.shape
    grid_spec = pltpu.PrefetchScalarGridSpec(
        num_scalar_prefetch=4,
        grid=(t // TM,),
        in_specs=[pl.BlockSpec((TM, d), lambda i, *_: (i, 0)),
                  pl.BlockSpec((None, 2, TM), lambda i, *_: (i, 0, 0))],
        out_specs=pl.BlockSpec(memory_space=pl.ANY),
        scratch_shapes=[pltpu.VMEM((2, MOE_BUF, d), f32), pltpu.VMEM((FFN_TM, d), f32),
                        pltpu.SemaphoreType.DMA((2,)), pltpu.SemaphoreType.DMA],
    )
    return pl.pallas_call(
        _moe_dispatch_kernel,
        grid_spec=grid_spec,
        out_shape=jax.ShapeDtypeStruct((n_tiles * FFN_TM, d), f32),
        compiler_params=_cparams(("arbitrary",)),
        name="moe_dispatch",
    )(*runs, *gaps, h, bufpos)


def _moe_ffn_kernel(te_ref, nu_ref, x_ref, wg_ref, wu_ref, wd_ref, y_ref):
    @pl.when(pl.program_id(0) < nu_ref[0])
    def _():
        x = x_ref[...].astype(bf16)
        y = None
        for c in range(0, wg_ref.shape[1], TM):
            a = _silu(_dot(x, wg_ref[:, c:c + TM])) * _dot(x, wu_ref[:, c:c + TM])
            part = _dot(a, wd_ref[c:c + TM, :])
            y = part if y is None else y + part
        y_ref[...] = y

    @pl.when(pl.program_id(0) >= nu_ref[0])
    def _():
        y_ref[...] = jnp.zeros(y_ref.shape, f32)


def _moe_ffn(xs, tile_e, n_used, wg, wu, wd, layer):
    p, d = xs.shape
    ff = wg.shape[3]
    row = lambda i, te, nu: (jnp.minimum(i, nu[0] - 1), 0)
    grid_spec = pltpu.PrefetchScalarGridSpec(
        num_scalar_prefetch=2,
        grid=(p // FFN_TM,),
        in_specs=[pl.BlockSpec((FFN_TM, d), row),
                  pl.BlockSpec((None, None, d, ff), lambda i, te, nu: (layer, te[i], 0, 0)),
                  pl.BlockSpec((None, None, d, ff), lambda i, te, nu: (layer, te[i], 0, 0)),
                  pl.BlockSpec((None, None, ff, d), lambda i, te, nu: (layer, te[i], 0, 0))],
        out_specs=pl.BlockSpec((FFN_TM, d), lambda i, te, nu: (i, 0)),
    )
    return pl.pallas_call(
        _moe_ffn_kernel,
        grid_spec=grid_spec,
        out_shape=jax.ShapeDtypeStruct((p, d), f32),
        compiler_params=_cparams(("arbitrary",)),
        name="moe_ffn",
    )(tile_e, n_used, xs, wg, wu, wd)


def _moe_combine_kernel(row_ref, ng_ref, y_hbm, x_ref, bp_ref, wt_ref, mod_ref, *rest, final):
    if final:
        fg_ref, o_ref, ybuf, sem = rest
    else:
        o_ref, ybuf, sem = rest
    runs = (row_ref, ng_ref)
    i = pl.program_id(0)
    n = pl.num_programs(0)
    slot = i % 2

    @pl.when(i == 0)
    def _():
        ybuf[...] = jnp.zeros(ybuf.shape, f32)
        _run_copies(runs, 0, ybuf.at[0], y_hbm, sem.at[0], to_hbm=False, start=True)

    @pl.when(i + 1 < n)
    def _():
        _run_copies(runs, i + 1, ybuf.at[1 - slot], y_hbm, sem.at[1 - slot], to_hbm=False, start=True)

    _run_copies(runs, i, ybuf.at[slot], y_hbm, sem.at[slot], to_hbm=False, start=False)
    bp = bp_ref[...]
    w = wt_ref[...]
    j = lax.broadcasted_iota(i32, (MOE_BUF, TM), 0)
    sel_t = jnp.where(j == bp[0:1], w[0:1], 0.0) + jnp.where(j == bp[1:2], w[1:2], 0.0)
    out = x_ref[...] + mod_ref[5:6] * _dot_tn(sel_t, ybuf[slot])
    if final:
        out = _rms_rows(out) * fg_ref[...]
    o_ref[...] = out


def _moe_combine(y, runs, bufpos, x, wts, modl, tiles_per_batch, lat_only, final_g=None):
    t, d = x.shape
    n = t // TM
    off = 1 if lat_only else 0
    final = final_g is not None
    in_specs = [pl.BlockSpec(memory_space=pl.ANY),
                pl.BlockSpec((TM, d), lambda i, *_: (i, 0)),
                pl.BlockSpec((None, 2, TM), lambda i, *_: (i, 0, 0)),
                pl.BlockSpec((2, TM), lambda i, *_: (0, i)),
                pl.BlockSpec((None, None, 6, d),
                             lambda i, *_: (i // tiles_per_batch, jnp.minimum(i % tiles_per_batch + off, 1), 0, 0))]
    args = [y, x, bufpos, wts, modl]
    if final:
        in_specs.append(pl.BlockSpec((1, d), lambda i, *_: (0, 0)))
        args.append(final_g)
    grid_spec = pltpu.PrefetchScalarGridSpec(
        num_scalar_prefetch=2,
        grid=(n,),
        in_specs=in_specs,
        out_specs=pl.BlockSpec((TM, d), lambda i, *_: (i, 0)),
        scratch_shapes=[pltpu.VMEM((2, MOE_BUF, d), f32), pltpu.SemaphoreType.DMA((2,))],
    )
    return pl.pallas_call(
        functools.partial(_moe_combine_kernel, final=final),
        grid_spec=grid_spec,
        out_shape=jax.ShapeDtypeStruct((t, d), f32),
        compiler_params=_cparams(("arbitrary",)),
        name="moe_combine_final" if final else "moe_combine",
    )(*runs, *args)


def _moe(h, x, idx, wts, modl, experts, layer, tiles_per_batch, lat_only, final_g=None):
    runs, gaps, bufpos, tile_e, n_used, n_tiles = _dispatch_plan(idx)
    xs = _moe_dispatch(h, runs, gaps, bufpos, n_tiles)
    y = _moe_ffn(xs, tile_e, n_used, *experts, layer)
    return _moe_combine(y, runs, bufpos, x, wts, modl, tiles_per_batch, lat_only, final_g)


def _cd_in_kernel(x_ref, mod_ref, g_ref, w_ref, ag_ref, qkv_ref, dg_ref, ab_ref):
    mod = mod_ref[...]
    h = _rms_rows(x_ref[...]) * g_ref[...] * (1.0 + mod[1:2]) + mod[0:1]
    acc = _dot(h, w_ref[...])
    c1 = 2 * CV_CH
    c2 = c1 + DN_QKV
    c3 = c2 + DN_QK
    ag_ref[...] = acc[:, 0:c1]
    qkv_ref[...] = acc[:, c1:c2]
    dg_ref[...] = acc[:, c2:c3]
    ab_ref[...] = acc[:, c3:c3 + LANES]


def _cd_in_proj(x, modl, g, w):
    b, l, d = x.shape
    nt = l // TM
    n = w.shape[1]
    full = lambda shape: pl.BlockSpec(shape, lambda bi, i: (0,) * len(shape))
    widths = (2 * CV_CH, DN_QKV, DN_QK, LANES)
    return pl.pallas_call(
        _cd_in_kernel,
        grid=(b, nt),
        in_specs=[pl.BlockSpec((None, TM, d), lambda bi, i: (bi, i, 0)),
                  pl.BlockSpec((None, None, 6, d), _mod_index),
                  full((1, d)), full((d, n))],
        out_specs=[pl.BlockSpec((None, TM, w_), lambda bi, i: (bi, i, 0)) for w_ in widths],
        out_shape=[jax.ShapeDtypeStruct((b, l, w_), f32) for w_ in widths],
        compiler_params=_cparams(("arbitrary", "arbitrary")),
        name="cd_in_proj",
    )(x, modl, g, w)


CV_HALO = 16
CV_ROWS = 32


def _conformer_kernel(prev_ref, cur_ref, next_ref, w_ref, b_ref, lng_ref, lnb_ref, o_ref, ext_ref, sh_ref):
    i = pl.program_id(1)
    last = pl.num_programs(1) - 1

    def glu(z):
        return z[:, 0:CV_CH] * _sigmoid(z[:, CV_CH:2 * CV_CH])

    ext_ref[0:CV_HALO, :] = jnp.where(i > 0, glu(prev_ref[...]), 0.0)
    ext_ref[CV_HALO:CV_HALO + TM, :] = glu(cur_ref[...])
    ext_ref[CV_HALO + TM:2 * CV_HALO + TM, :] = jnp.where(i < last, glu(next_ref[...]), 0.0)
    n_sh = sh_ref.shape[1]
    for r in range(1, SUBLANES):
        sh_ref[r] = ext_ref[pl.ds(r, n_sh), :]
    base = CV_HALO - CV_WIDTH // 2
    groups = CV_ROWS // SUBLANES
    for rb in range(TM // CV_ROWS):
        acc = jnp.zeros((groups, SUBLANES, CV_CH), f32)
        for j in range(CV_WIDTH):
            a, r = divmod(base + j, SUBLANES)
            start = rb * CV_ROWS + a * SUBLANES
            x = ext_ref[pl.ds(start, CV_ROWS), :] if r == 0 else sh_ref[r, pl.ds(start, CV_ROWS), :]
            acc = acc + x.reshape(groups, SUBLANES, CV_CH) * w_ref[j][None]
        y = _layer_norm_rows(acc.reshape(CV_ROWS, CV_CH) + b_ref[...], lng_ref[...], lnb_ref[...])
        o_ref[rb * CV_ROWS:(rb + 1) * CV_ROWS, :] = _silu(y).astype(bf16)


def _conformer(ag, w, bias, lng, lnb):
    b, l, c2 = ag.shape
    nt = l // TM - 1
    hb = TM // CV_HALO
    n_halo = l // CV_HALO
    full = lambda shape: pl.BlockSpec(shape, lambda bi, i: (0,) * len(shape))
    return pl.pallas_call(
        _conformer_kernel,
        grid=(b, nt),
        in_specs=[pl.BlockSpec((None, CV_HALO, c2), lambda bi, i: (bi, (i + 1) * hb - 1, 0)),
                  pl.BlockSpec((None, TM, c2), lambda bi, i: (bi, i + 1, 0)),
                  pl.BlockSpec((None, CV_HALO, c2), lambda bi, i: (bi, jnp.minimum((i + 2) * hb, n_halo - 1), 0)),
                  full((CV_WIDTH, SUBLANES, CV_CH)), full((1, CV_CH)), full((1, CV_CH)), full((1, CV_CH))],
        out_specs=pl.BlockSpec((None, TM, CV_CH), lambda bi, i: (bi, i, 0)),
        out_shape=jax.ShapeDtypeStruct((b, nt * TM, CV_CH), bf16),
        scratch_shapes=[pltpu.VMEM((TM + 2 * CV_HALO, CV_CH), f32),
                        pltpu.VMEM((SUBLANES, TM + 2 * CV_HALO - SUBLANES, CV_CH), f32)],
        compiler_params=_cparams(("arbitrary", "arbitrary")),
        name="conformer",
    )(ag, ag, ag, w, bias, lng, lnb)


DN_HALO = 8


def _delta_feat_kernel(prev_ref, cur_ref, next_ref, ab_ref, w_ref, alog_ref, dtb_ref,
                       q_ref, k_ref, v_ref, g_ref, ext_ref):
    i = pl.program_id(1)
    last = pl.num_programs(1) - 1
    ext_ref[0:DN_HALO, :] = jnp.where(i > 1, prev_ref[...], 0.0)
    ext_ref[DN_HALO:DN_HALO + TM, :] = cur_ref[...]
    ext_ref[DN_HALO + TM:2 * DN_HALO + TM, :] = jnp.where((i > 0) & (i < last), next_ref[...], 0.0)
    w = w_ref[...]
    base = DN_HALO - DN_CONV // 2
    outs = (q_ref, k_ref, v_ref)
    for c in range(DN_QKV // LANES):
        cs = slice(c * LANES, (c + 1) * LANES)
        acc = jnp.zeros((TM, LANES), f32)
        for j in range(DN_CONV):
            acc = acc + ext_ref[pl.ds(base + j, TM), cs] * w[j:j + 1, cs]
        y = _silu(acc)
        part, h = divmod(c, DN_HEADS)
        if part < 2:
            y = y * lax.rsqrt(jnp.sum(y * y, axis=-1, keepdims=True) + EPS)
            if part == 0:
                y = y * (DN_DK ** -0.5)
        outs[part][:, h * LANES:(h + 1) * LANES] = y

    ab = ab_ref[...]
    g = -jnp.exp(alog_ref[...]) * (jnp.maximum(ab + dtb_ref[...], 0.0)
                                   + jnp.log(1.0 + jnp.exp(-jnp.abs(ab + dtb_ref[...]))))
    beta = _sigmoid(ab)
    lane = lax.broadcasted_iota(i32, ab.shape, 1)
    row = lax.broadcasted_iota(i32, (TM, TM), 0)
    col = lax.broadcasted_iota(i32, (TM, TM), 1)
    same = (row // DN_CHUNK) == (col // DN_CHUNK)
    for d in range(N_DIR):
        gd = g if d == 0 else pltpu.roll(g, LANES - DN_HEADS, 1)
        bd = pltpu.roll(beta, LANES - DN_HEADS * (d + 1), 1)
        tri = (same & ((col <= row) if d == 0 else (col >= row))).astype(f32)
        gc = jnp.dot(tri, jnp.where(lane < DN_HEADS, gd, 0.0), precision=lax.Precision.HIGHEST,
                     preferred_element_type=f32)
        g_ref[d] = jnp.where(lane < DN_HEADS, gc, bd)


def _delta_features(qkv, ab, w, alog, dtb):
    b, l, c = qkv.shape
    nt = l // TM
    hb = TM // DN_HALO
    n_halo = l // DN_HALO
    full = lambda shape: pl.BlockSpec(shape, lambda bi, i: (0,) * len(shape))
    return pl.pallas_call(
        _delta_feat_kernel,
        grid=(b, nt),
        in_specs=[pl.BlockSpec((None, DN_HALO, c), lambda bi, i: (bi, jnp.maximum(i * hb - 1, 0), 0)),
                  pl.BlockSpec((None, TM, c), lambda bi, i: (bi, i, 0)),
                  pl.BlockSpec((None, DN_HALO, c), lambda bi, i: (bi, jnp.minimum((i + 1) * hb, n_halo - 1), 0)),
                  pl.BlockSpec((None, TM, LANES), lambda bi, i: (bi, i, 0)),
                  full((DN_CONV, c)), full((1, LANES)), full((1, LANES))],
        out_specs=[pl.BlockSpec((None, TM, DN_QK), lambda bi, i: (bi, i, 0))] * 3
        + [pl.BlockSpec((N_DIR, None, TM, LANES), lambda bi, i: (0, bi, i, 0))],
        out_shape=[jax.ShapeDtypeStruct((b, l, DN_QK), f32)] * 3
        + [jax.ShapeDtypeStruct((N_DIR, b, l, LANES), f32)],
        scratch_shapes=[pltpu.VMEM((TM + 2 * DN_HALO, c), f32)],
        compiler_params=_cparams(("arbitrary", "arbitrary")),
        name="delta_features",
    )(qkv, qkv, qkv, ab, w, alog, dtb)


DN_PRE_CHUNKS = 4
DN_EG_ROWS = 8


def _delta_pre_kernel(q_ref, k_ref, v_ref, g_ref, u_ref, w_ref, qe_ref, kd_ref, qk_ref, eg_ref):
    n = DN_CHUNK
    row = lax.broadcasted_iota(i32, (n, LANES), 0)
    col = lax.broadcasted_iota(i32, (n, LANES), 1)
    eye = row == col
    valid = col < n
    zpad = jnp.zeros((LANES - n, LANES), f32)
    sh = [(s, h) for s in range(DN_PRE_CHUNKS) for h in range(DN_HEADS)]
    rsl = lambda s: slice(s * n, (s + 1) * n)
    csl = lambda h: slice(h * LANES, (h + 1) * LANES)
    q = [q_ref[rsl(s), csl(h)] for s, h in sh]
    k = [k_ref[rsl(s), csl(h)] for s, h in sh]
    v = [v_ref[rsl(s), csl(h)] for s, h in sh]
    k_pad = [jnp.concatenate([ki, zpad], axis=0) for ki in k]
    kk = [_dot_nt(ki, kp) for ki, kp in zip(k, k_pad)]
    qk = [_dot_nt(qi, kp) for qi, kp in zip(q, k_pad)]
    prob = [(i, d) for i in range(len(sh)) for d in range(N_DIR)]
    gc, beta, gl, decay, a = [], [], [], [], []
    for i, d in prob:
        s, h = sh[i]
        gates = g_ref[d, rsl(s), :]
        gci = gates[:, h:h + 1]
        bi = gates[:, DN_HEADS + h:DN_HEADS + h + 1]
        last = n - 1 if d == 0 else 0
        incl = ((col <= row) if d == 0 else (col >= row)) & valid
        strict = ((col < row) if d == 0 else (col > row)) & valid
        gc_row = jnp.sum(jnp.where(eye, gci, 0.0), axis=0, keepdims=True)
        di = jnp.exp(jnp.where(incl, gci - gc_row, -jnp.inf))
        gc.append(gci)
        beta.append(bi)
        gl.append(gates[last:last + 1, h:h + 1])
        decay.append(di)
        a.append(jnp.where(strict, -(bi * kk[i] * di), 0.0))
    y = list(a)
    p = list(a)
    for _ in range(5):
        p = [_dot(pi[:, 0:n], pi) for pi in p]
        yp = [_dot(yi[:, 0:n], pi) for yi, pi in zip(y, p)]
        y = [yi + pi + ypi for yi, pi, ypi in zip(y, p, yp)]
    e_gc = [jnp.exp(g) for g in gc]
    rhs = [jnp.concatenate([v[i] * beta[j], k[i] * (beta[j] * e_gc[j])], axis=1) for j, (i, d) in enumerate(prob)]
    sol = [r + _dot(yi[:, 0:n], r) for yi, r in zip(y, rhs)]
    for j, (i, d) in enumerate(prob):
        s, h = sh[i]
        rs, cs = rsl(s), csl(h)
        u_ref[d, rs, cs] = sol[j][:, 0:LANES]
        w_ref[d, rs, cs] = sol[j][:, LANES:2 * LANES].astype(bf16)
        qe_ref[d, rs, cs] = (q[i] * e_gc[j]).astype(bf16)
        kd_ref[d, rs, cs] = (k[i] * jnp.exp(gl[j] - gc[j])).astype(bf16)
        qk_ref[d, rs, cs] = (qk[i] * decay[j]).astype(bf16)
    for s in range(DN_PRE_CHUNKS):
        for d in range(N_DIR):
            last = s * n + (n - 1 if d == 0 else 0)
            eg_ref[d, s * DN_EG_ROWS:(s + 1) * DN_EG_ROWS, :] = jnp.broadcast_to(
                jnp.exp(g_ref[d, last:last + 1, :]), (DN_EG_ROWS, LANES))


def _delta_pre(q, k, v, gates):
    b, l, c = q.shape
    rows = DN_PRE_CHUNKS * DN_CHUNK
    n = l // rows
    row = pl.BlockSpec((None, rows, c), lambda bi, i: (bi, i, 0))
    out = pl.BlockSpec((N_DIR, None, rows, c), lambda bi, i: (0, bi, i, 0))
    eg_rows = DN_PRE_CHUNKS * DN_EG_ROWS
    return pl.pallas_call(
        _delta_pre_kernel,
        grid=(b, n),
        in_specs=[row, row, row, pl.BlockSpec((N_DIR, None, rows, LANES), lambda bi, i: (0, bi, i, 0))],
        out_specs=[out] * 5 + [pl.BlockSpec((N_DIR, None, eg_rows, LANES), lambda bi, i: (0, bi, i, 0))],
        out_shape=[jax.ShapeDtypeStruct((N_DIR, b, l, c), f32)]
        + [jax.ShapeDtypeStruct((N_DIR, b, l, c), bf16)] * 4
        + [jax.ShapeDtypeStruct((N_DIR, b, n * eg_rows, LANES), f32)],
        compiler_params=_cparams(("arbitrary", "arbitrary")),
        name="delta_pre",
    )(q, k, v, gates)


def _delta_scan_kernel(*refs):
    n_in = 6 * N_DIR
    o_refs = refs[n_in:n_in + N_DIR]
    s_ref = refs[n_in + N_DIR]
    n = DN_CHUNK
    n_sub = TM // DN_CHUNK

    @pl.when(pl.program_id(1) == 0)
    def _():
        s_ref[...] = jnp.zeros(s_ref.shape, f32)

    dh = [(d, h) for d in range(N_DIR) for h in range(DN_HEADS)]
    csl = lambda h: slice(h * LANES, (h + 1) * LANES)
    ins = lambda d: refs[6 * d:6 * d + 6]
    s = [s_ref[d, h] for d, h in dh]
    for c in range(n_sub):
        sub = lambda d: c if d == 0 else n_sub - 1 - c
        rsl = lambda d: slice(sub(d) * n, (sub(d) + 1) * n)
        r = [_dot(jnp.concatenate([ins(d)[1][rsl(d), csl(h)], ins(d)[2][rsl(d), csl(h)]], axis=0), si)
             for (d, h), si in zip(dh, s)]
        v_new = [ins(d)[0][rsl(d), csl(h)] - ri[0:n] for (d, h), ri in zip(dh, r)]
        s_add = [_dot_tn(ins(d)[3][rsl(d), csl(h)], vi) for (d, h), vi in zip(dh, v_new)]
        o_add = [_dot(ins(d)[4][rsl(d), h * LANES:h * LANES + n], vi) for (d, h), vi in zip(dh, v_new)]
        for j, (d, h) in enumerate(dh):
            o_refs[d][rsl(d), csl(h)] = r[j][n:2 * n] + o_add[j]
        s = [s[j] * ins(d)[5][sub(d) * DN_EG_ROWS:sub(d) * DN_EG_ROWS + 1, h:h + 1] + s_add[j]
             for j, (d, h) in enumerate(dh)]
    for j, (d, h) in enumerate(dh):
        s_ref[d, h] = s[j]


def _delta_scan(u, w, qe, kd, qk, eg):
    _, b, l, c = u.shape
    nt = l // TM
    eg_rows = TM // DN_CHUNK * DN_EG_ROWS

    def tile(d, ti):
        if d == 0:
            return ti
        return jnp.where(ti == 0, 0, nt - ti)

    in_specs, args = [], []
    for d in range(N_DIR):
        spec = pl.BlockSpec((None, None, TM, c), lambda bi, ti, d=d: (d, bi, tile(d, ti), 0))
        in_specs += [spec] * 5 + [pl.BlockSpec((None, None, eg_rows, LANES),
                                               lambda bi, ti, d=d: (d, bi, tile(d, ti), 0))]
        args += [u, w, qe, kd, qk, eg]
    return pl.pallas_call(
        _delta_scan_kernel,
        grid=(b, nt),
        in_specs=in_specs,
        out_specs=[pl.BlockSpec((None, TM, c), lambda bi, ti, d=d: (bi, tile(d, ti), 0))
                   for d in range(N_DIR)],
        out_shape=[jax.ShapeDtypeStruct((b, l, c), f32)] * N_DIR,
        scratch_shapes=[pltpu.VMEM((N_DIR, DN_HEADS, DN_DK, LANES), f32)],
        compiler_params=_cparams(("arbitrary", "arbitrary")),
        name="delta_scan",
    )(*args)


def _rope_tables(s, ctx_len):
    rows = s // GRID_W
    row = jnp.repeat(jnp.arange(rows, dtype=i32), GRID_W)
    col = jnp.tile(jnp.arange(GRID_W, dtype=i32), rows)
    freqs = ROPE_THETA ** (-jnp.arange(ROPE_PAIRS, dtype=f32) / ROPE_PAIRS)
    ang = jnp.stack([row, col], axis=-1).astype(f32)[..., None] * freqs
    cos = jnp.cos(ang)
    sin = jnp.sin(ang)
    cos_h = jnp.concatenate([cos[:, 0], cos[:, 0], cos[:, 1], cos[:, 1]], axis=-1)
    sin_h = jnp.concatenate([-sin[:, 0], sin[:, 0], -sin[:, 1], sin[:, 1]], axis=-1)
    cos_t = jnp.concatenate([jnp.ones((ctx_len, HEAD_DIM), f32), cos_h], axis=0)
    sin_t = jnp.concatenate([jnp.zeros((ctx_len, HEAD_DIM), f32), sin_h], axis=0)
    return jnp.tile(cos_t, (1, 2)), jnp.tile(sin_t, (1, 2))


def _pad_lanes(v, n=LANES):
    v = v.reshape(1, -1)
    return jnp.pad(v, ((0, 0), (0, n - v.shape[1])))


def kernel(x, c, ctx, c_ctx, mod_w, mod_b, norm1_g, norm2_g, ab_w_in, ab_q_norm, ab_k_norm, gm_ln_g, gm_ln_b, gm_w_s, gm_b_s, ab_w_out, cd_w_in, cv_dw_w, cv_dw_b, cv_ln_g, cv_ln_b, dn_conv_w, dn_a_log, dn_dt_bias, dn_o_norm, cd_w_out, router_w, router_b, moe_w_gate, moe_w_up, moe_w_down, final_norm_g):
    b, s, d = x.shape
    ctx_len = ctx.shape[1]
    assert ctx_len == TM and s % (4 * TM) == 0 and mod_w.shape[0] == 2
    l = ctx_len + s
    nt = l // TM

    n_rows = -(-(b + 1) // 8) * 8
    cc = jnp.concatenate([c, c_ctx[None, :], jnp.zeros((n_rows - b - 1, d), f32)], axis=0)
    mod = _modulation(cc, mod_w, mod_b)
    mod_lat = mod[:, :b].reshape(2, b, 1, 6, d)
    mod_ctx = jnp.broadcast_to(mod[:, b].reshape(2, 1, 1, 6, d), (2, b, 1, 6, d))
    modl = jnp.concatenate([mod_ctx, mod_lat], axis=2)

    rw32 =jnp.pad(router_w, ((0, 0), (0, LANES - N_EXPERTS)))
    rw_hi = rw32.astype(bf16)
    rw = jnp.concatenate([rw_hi, (rw32 - rw_hi.astype(f32)).astype(bf16)], axis=1)
    rb = router_b.reshape(N_EXPERTS, 1)
    experts = (moe_w_gate, moe_w_up, moe_w_down)

    cos_t, sin_t = _rope_tables(s, ctx_len)
    lane = jnp.arange(2 * LANES)
    seg = ((lane[:, None] // HEAD_DIM) == (lane[None, :] // HEAD_DIM)).astype(bf16) * (1.0 / HEAD_DIM)
    q, kd, vt, sg = _ab_in_proj(
        ctx, x, modl[0], norm1_g[0:1], ab_w_in[0].astype(bf16), seg,
        jnp.tile(ab_q_norm[0], 2)[None, :], jnp.tile(ab_k_norm[0], 2)[None, :], cos_t, sin_t,
        gm_ln_g[0:1], gm_ln_b[0:1], gm_w_s[0].astype(bf16),
        jnp.broadcast_to(gm_b_s[0][:, :, None], (GM_GROUPS, GM_CHUNK, LANES)))
    att = _attention(q, kd, vt)
    x1, h1, idx, wts = _out_proj(att, sg, (ctx, x), modl[0], norm2_g[0:1], ab_w_out[0].astype(bf16), rw, rb,
                                 delta=False)
    xs = _moe(h1, x1, idx, wts, modl[0], experts, 0, nt, False).reshape(b, l, d)

    c1 = 2 * CV_CH
    c2 = c1 + DN_QKV
    c3 = c2 + 2 * N_DIR * DN_HEADS
    w_in = cd_w_in[0]
    w_cd = jnp.concatenate([w_in[:, 0:c2], w_in[:, c3:], w_in[:, c2:c3],
                            jnp.zeros((d, LANES - (c3 - c2)), f32)], axis=1).astype(bf16)
    ag, qkv, dg, ab = _cd_in_proj(xs, modl[1], norm1_g[1:2], w_cd)
    conv_w = jnp.broadcast_to(cv_dw_w[0][:, None, :], (CV_WIDTH, SUBLANES, CV_CH))
    conv = _conformer(ag, conv_w, cv_dw_b[0:1], cv_ln_g[0:1], cv_ln_b[0:1])
    qn, kn, vv, gates = _delta_features(qkv, ab, dn_conv_w[0], _pad_lanes(dn_a_log[0]), _pad_lanes(dn_dt_bias[0]))
    o_f, o_b = _delta_scan(*_delta_pre(qn, kn, vv, gates))
    x2, h2, idx, wts = _out_proj(conv, (o_f, o_b, dg, dn_o_norm[0:1]), xs, modl[1], norm2_g[1:2],
                                 cd_w_out[0].astype(bf16), rw, rb, delta=True)
    out = _moe(h2, x2, idx, wts, modl[1], experts, 1, nt - 1, True, final_norm_g[None, :])
    return out.reshape(b, s, d)
```

```python
import functools
import math

import jax
import jax.numpy as jnp
from jax import lax
from jax.experimental import pallas as pl
from jax.experimental.pallas import tpu as pltpu

f32 = jnp.float32
bf16 = jnp.bfloat16
i32 = jnp.int32

GRID_W = 64
EPS = 1e-6
ATT_HEADS = 8
ATT_KV_HEADS = 2
HEAD_DIM = 64
ROPE_PAIRS = HEAD_DIM // 4
ROPE_THETA = 10000.0
A_Q = ATT_HEADS * HEAD_DIM
A_KV = ATT_KV_HEADS * HEAD_DIM
GM_GROUPS = 4
GM_CHUNK = 128
GM_CH = 512
CV_CH = 512
CV_WIDTH = 31
DN_HEADS = 4
DN_DK = 128
DN_CONV = 5
DN_CHUNK = 64
N_DIR = 2
DN_QK = DN_HEADS * DN_DK
DN_QKV = 3 * DN_QK
N_EXPERTS = 16
N_GROUPS = 4
EXPERT_FF = 512

LANES = 128
SUBLANES = 8
TM = 256
V_ROWS = HEAD_DIM + 16
VMEM_LIMIT = 48 * 1024 * 1024


def _cparams(sem):
    return pltpu.CompilerParams(dimension_semantics=sem, vmem_limit_bytes=VMEM_LIMIT)


def _sigmoid(x):
    return 1.0 / (1.0 + jnp.exp(-x))


def _silu(x):
    return x * _sigmoid(x)


def _gelu(x):
    return x * (0.5 * (1.0 + jnp.tanh(math.sqrt(2.0 / math.pi) * (x + 0.044715 * (x * x * x)))))


def _rms_rows(x):
    return x * lax.rsqrt(jnp.mean(x * x, axis=-1, keepdims=True) + EPS)


def _layer_norm_rows(x, g, b):
    mu = jnp.mean(x, axis=-1, keepdims=True)
    xc = x - mu
    var = jnp.mean(xc * xc, axis=-1, keepdims=True)
    return xc * lax.rsqrt(var + EPS) * g + b


def _dot(a, b):
    return jnp.dot(a.astype(bf16), b.astype(bf16), preferred_element_type=f32)


def _dot_nt(a, b):
    return lax.dot_general(a.astype(bf16), b.astype(bf16), (((1,), (1,)), ((), ())), preferred_element_type=f32)


def _dot_tn(a, b):
    return lax.dot_general(a.astype(bf16), b.astype(bf16), (((0,), (0,)), ((), ())), preferred_element_type=f32)


def _mod_kernel(c_ref, w_ref, b_ref, o_ref):
    c = c_ref[...]
    o_ref[...] = jnp.dot(_silu(c), w_ref[...], precision=lax.Precision.HIGHEST,
                         preferred_element_type=f32) + b_ref[...]


def _modulation(cc, mod_w, mod_b):
    depth, d, n = mod_w.shape
    tn = 1024
    rows = cc.shape[0]
    return pl.pallas_call(
        _mod_kernel,
        grid=(depth, n // tn),
        in_specs=[pl.BlockSpec((rows, d), lambda l, j: (0, 0)),
                  pl.BlockSpec((None, d, tn), lambda l, j: (l, 0, j)),
                  pl.BlockSpec((None, 1, tn), lambda l, j: (l, 0, j))],
        out_specs=pl.BlockSpec((None, rows, tn), lambda l, j: (l, 0, j)),
        out_shape=jax.ShapeDtypeStruct((depth, rows, n), f32),
        compiler_params=_cparams(("arbitrary", "arbitrary")),
        name="modulation",
    )(cc, mod_w, mod_b.reshape(depth, 1, n))


def _mod_index(b, i):
    return (b, jnp.minimum(i, 1), 0, 0)


def _head_norm_rope(x, seg_mean, gain, cos, sin, scale):
    xx = x * x
    outs = []
    lane = lax.broadcasted_iota(i32, (x.shape[0], LANES), 1)
    first = (lane % 32) < 16
    for c0 in range(0, x.shape[1], 2 * LANES):
        w = min(2 * LANES, x.shape[1] - c0)
        ms = _dot(xx[:, c0:c0 + w], seg_mean[0:w, 0:w])
        xn = x[:, c0:c0 + w] * lax.rsqrt(ms + EPS)
        for c1 in range(0, w, LANES):
            y = xn[:, c1:c1 + LANES] * gain
            partner = jnp.where(first, pltpu.roll(y, LANES - 16, 1), pltpu.roll(y, 16, 1))
            outs.append((y * cos + partner * sin) * scale)
    return outs


def _tile_rows(ctx_ref, x_ref):
    return jnp.where(pl.program_id(1) == 0, ctx_ref[...], x_ref[...])


def _ctx_lat_specs(d):
    return [pl.BlockSpec((None, TM, d), lambda bi, i: (bi, 0, 0)),
            pl.BlockSpec((None, TM, d), lambda bi, i: (bi, jnp.maximum(i - 1, 0), 0))]


def _ab_in_kernel(ctx_ref, x_ref, mod_ref, g_ref, w_ref, seg_ref, qg_ref, kg_ref, cos_ref, sin_ref,
                  lng_ref, lnb_ref, ws_ref, bs_ref, q_ref, kd_ref, vt_ref, s_ref):
    mod = mod_ref[...]
    h = _rms_rows(_tile_rows(ctx_ref, x_ref)) * g_ref[...] * (1.0 + mod[1:2]) + mod[0:1]
    acc = _dot(h, w_ref[...])
    cos = cos_ref[...]
    sin = sin_ref[...]
    seg = seg_ref[...]

    q = _head_norm_rope(acc[:, 0:A_Q], seg, qg_ref[...], cos, sin, HEAD_DIM ** -0.5 * math.log2(math.e))
    for c, qc in enumerate(q):
        q_ref[:, c * LANES:(c + 1) * LANES] = qc.astype(bf16)

    (k,) = _head_norm_rope(acc[:, A_Q:A_Q + A_KV], seg, kg_ref[...], cos, sin, 1.0)
    lane = lax.broadcasted_iota(i32, k.shape, 1)
    swapped = pltpu.roll(k, HEAD_DIM, 1)
    kd_ref[0] = jnp.where(lane < HEAD_DIM, k, swapped).astype(bf16)
    kd_ref[1] = jnp.where(lane < HEAD_DIM, swapped, k).astype(bf16)

    vt = acc[:, A_Q + A_KV:A_Q + 2 * A_KV].T
    ones = jnp.ones((V_ROWS - HEAD_DIM, vt.shape[1]), bf16)
    for j in range(ATT_KV_HEADS):
        vt_ref[j, 0:HEAD_DIM, :] = vt[j * HEAD_DIM:(j + 1) * HEAD_DIM].astype(bf16)
        vt_ref[j, HEAD_DIM:V_ROWS, :] = ones

    c0 = A_Q + 2 * A_KV
    gu = _gelu(acc[:, c0:c0 + GM_CH])
    ln = _layer_norm_rows(_gelu(acc[:, c0 + GM_CH:c0 + 2 * GM_CH]), lng_ref[...], lnb_ref[...])
    for n in range(acc.shape[0] // GM_CHUNK):
        r = slice(n * GM_CHUNK, (n + 1) * GM_CHUNK)
        for g in range(GM_GROUPS):
            cs = slice(g * LANES, (g + 1) * LANES)
            mixed = _dot(ws_ref[g], ln[r, cs]) + bs_ref[g]
            s_ref[r, cs] = (gu[r, cs] * mixed).astype(bf16)


def _ab_in_proj(ctx, x, modl, g, w, seg, qg, kg, cos, sin, lng, lnb, ws, bs):
    b, s, d = x.shape
    l = s + ctx.shape[1]
    nt = l // TM
    n = w.shape[1]
    full = lambda shape: pl.BlockSpec(shape, lambda bi, i: (0,) * len(shape))
    return pl.pallas_call(
        _ab_in_kernel,
        grid=(b, nt),
        in_specs=_ctx_lat_specs(d) + [
                  pl.BlockSpec((None, None, 6, d), _mod_index),
                  full((1, d)), full((d, n)), full((2 * LANES, 2 * LANES)),
                  full((1, LANES)), full((1, LANES)),
                  pl.BlockSpec((TM, LANES), lambda bi, i: (i, 0)),
                  pl.BlockSpec((TM, LANES), lambda bi, i: (i, 0)),
                  full((1, GM_CH)), full((1, GM_CH)),
                  full((GM_GROUPS, GM_CHUNK, GM_CHUNK)), full((GM_GROUPS, GM_CHUNK, LANES))],
        out_specs=[pl.BlockSpec((None, TM, A_Q), lambda bi, i: (bi, i, 0)),
                   pl.BlockSpec((None, ATT_KV_HEADS, TM, LANES), lambda bi, i: (bi, 0, i, 0)),
                   pl.BlockSpec((None, ATT_KV_HEADS, None, V_ROWS, TM), lambda bi, i: (bi, 0, i, 0, 0)),
                   pl.BlockSpec((None, TM, GM_CH), lambda bi, i: (bi, i, 0))],
        out_shape=[jax.ShapeDtypeStruct((b, l, A_Q), bf16),
                   jax.ShapeDtypeStruct((b, ATT_KV_HEADS, l, LANES), bf16),
                   jax.ShapeDtypeStruct((b, ATT_KV_HEADS, nt, V_ROWS, TM), bf16),
                   jax.ShapeDtypeStruct((b, l, GM_CH), bf16)],
        compiler_params=_cparams(("arbitrary", "arbitrary")),
        name="ab_in_proj",
    )(ctx, x, modl, g, w, seg, qg, kg, cos, sin, lng, lnb, ws, bs)


def _attn_kernel(q_ref, k_ref, vt_ref, o_ref, acc_ref, m_ref, st_ref, st2_ref, *, n_pairs):
    i = pl.program_id(2)
    q = q_ref[...]
    lane = lax.broadcasted_iota(i32, (TM, LANES), 1)
    zero = jnp.zeros((TM, LANES), bf16)
    qs = []
    for p in range(2):
        qp = q[:, p * LANES:(p + 1) * LANES]
        qs.append(jnp.where(lane < HEAD_DIM, qp, zero))
        qs.append(jnp.where(lane < HEAD_DIM, zero, qp))
    heads = range(len(qs))
    hsl = lambda h: slice(h * TM, (h + 1) * TM)
    m_ref[...] = jnp.full(m_ref.shape, -jnp.inf, f32)
    acc_ref[...] = jnp.zeros(acc_ref.shape, f32)

    def scores(kb, h):
        return _dot_nt(kb, qs[h])

    def update(st, vb, h):
        m_old = m_ref[:, hsl(h)]
        m_new = jnp.maximum(m_old, jnp.max(st, axis=0, keepdims=True))
        p = jnp.exp2(st - m_new).astype(bf16)
        acc_ref[:, hsl(h)] = (acc_ref[:, hsl(h)] * jnp.exp2(m_old - m_new)
                              + jnp.dot(vb, p, preferred_element_type=f32))
        m_ref[:, hsl(h)] = m_new

    def latent_keys(t):
        return k_ref[pl.ds(pl.multiple_of(TM + t * TM, TM), TM), :]

    def latent_values(t):
        return vt_ref[1 + t]

    kb0 = k_ref[0:TM, :]

    @pl.when(i == 0)
    def _():
        for h in heads:
            update(scores(kb0, h), vt_ref[0], h)

    @pl.when(i > 0)
    def _():
        def step(t, cur, nxt):
            kb_next = latent_keys(t + 1)
            vb = latent_values(t)
            for h in heads:
                nxt[:, hsl(h)] = scores(kb_next, h)
                update(cur[:, hsl(h)], vb, h)

        kb = latent_keys(0)
        st_c = [scores(kb0, h) for h in heads]
        for h in heads:
            st_ref[:, hsl(h)] = scores(kb, h)
            update(st_c[h], vt_ref[0], h)

        for j in range((n_pairs - 2) // 2):
            step(2 * j, st_ref, st2_ref)
            step(2 * j + 1, st2_ref, st_ref)
        step(n_pairs - 2, st_ref, st2_ref)
        vb = latent_values(n_pairs - 1)
        for h in heads:
            update(st2_ref[:, hsl(h)], vb, h)

    acc = acc_ref[...]
    o = acc[0:HEAD_DIM] / acc[HEAD_DIM:HEAD_DIM + 1]
    for p in range(2):
        pair = jnp.concatenate([o[:, (2 * p) * TM:(2 * p + 1) * TM],
                                o[:, (2 * p + 1) * TM:(2 * p + 2) * TM]], axis=0)
        o_ref[:, p * LANES:(p + 1) * LANES] = pair.T.astype(bf16)


def _attention(q, kd, vt):
    b, l, _ = q.shape
    nt = l // TM
    group_w = A_Q // ATT_KV_HEADS
    assert (l - TM) % (4 * TM) == 0
    return pl.pallas_call(
        functools.partial(_attn_kernel, n_pairs=(l - TM) // TM),
        grid=(b, ATT_KV_HEADS, nt),
        in_specs=[pl.BlockSpec((None, TM, group_w), lambda bi, j, i: (bi, i, j)),
                  pl.BlockSpec((None, None, l, LANES), lambda bi, j, i: (bi, j, 0, 0)),
                  pl.BlockSpec((None, None, nt, V_ROWS, TM), lambda bi, j, i: (bi, j, 0, 0, 0))],
        out_specs=pl.BlockSpec((None, TM, group_w), lambda bi, j, i: (bi, i, j)),
        out_shape=jax.ShapeDtypeStruct((b, l, A_Q), bf16),
        scratch_shapes=[pltpu.VMEM((V_ROWS, 4 * TM), f32), pltpu.VMEM((1, 4 * TM), f32),
                        pltpu.VMEM((TM, 4 * TM), f32), pltpu.VMEM((TM, 4 * TM), f32)],
        compiler_params=_cparams(("arbitrary", "arbitrary", "arbitrary")),
        name="attention",
    )(q, kd, vt)


def _route_rows(scores, sel):
    s = [sel[e:e + 1] for e in range(N_EXPERTS)]
    sc = [scores[e:e + 1] for e in range(N_EXPERTS)]
    per = N_EXPERTS // N_GROUPS
    gs = []
    for g in range(N_GROUPS):
        a, b, c, d = s[per * g:per * g + per]
        hi1, lo1 = jnp.maximum(a, b), jnp.minimum(a, b)
        hi2, lo2 = jnp.maximum(c, d), jnp.minimum(c, d)
        gs.append(jnp.maximum(hi1, hi2) + jnp.maximum(jnp.minimum(hi1, hi2), jnp.maximum(lo1, lo2)))
    best = jnp.zeros(gs[0].shape, i32)
    best_v = gs[0]
    for g in range(1, N_GROUPS):
        better = gs[g] > best_v
        best = jnp.where(better, g, best)
        best_v = jnp.where(better, gs[g], best_v)
    v, w = [], []
    for j in range(per):
        vj, wj = s[j], sc[j]
        for g in range(1, N_GROUPS):
            vj = jnp.where(best == g, s[per * g + j], vj)
            wj = jnp.where(best == g, sc[per * g + j], wj)
        v.append(vj)
        w.append(wj)
    i1 = jnp.zeros(best.shape, i32)
    m1, w1 = v[0], w[0]
    for j in range(1, per):
        better = v[j] > m1
        i1 = jnp.where(better, j, i1)
        m1 = jnp.where(better, v[j], m1)
        w1 = jnp.where(better, w[j], w1)
    i2 = jnp.zeros(best.shape, i32)
    m2 = jnp.full(m1.shape, -jnp.inf, f32)
    w2 = jnp.zeros(m1.shape, f32)
    for j in range(per):
        cand = jnp.where(i1 == j, -jnp.inf, v[j])
        better = cand > m2
        i2 = jnp.where(better, j, i2)
        m2 = jnp.where(better, cand, m2)
        w2 = jnp.where(better, w[j], w2)
    tot = w1 + w2
    return best * per + i1, best * per + i2, w1 / tot, w2 / tot


def _out_kernel(*refs, delta):
    if delta:
        (a1_ref, of_ref, ob_ref, dg_ref, on_ref, x_ref, mod_ref, g_ref, w_ref, rw_ref, rb_ref,
         xo_ref, h_ref, idx_ref, wt_ref) = refs
        o = of_ref[...] + ob_ref[...]
        dg = dg_ref[...]
        parts = []
        for h in range(DN_HEADS):
            cs = slice(h * LANES, (h + 1) * LANES)
            parts.append(_rms_rows(o[:, cs]) * on_ref[...] * _silu(dg[:, cs]))
        a2 = jnp.concatenate(parts, axis=1)
        x = x_ref[...]
    else:
        (a1_ref, a2_ref, ctx_ref, x_ref, mod_ref, g_ref, w_ref, rw_ref, rb_ref,
         xo_ref, h_ref, idx_ref, wt_ref) = refs
        a2 = a2_ref[...]
        x = _tile_rows(ctx_ref, x_ref)
    half = a1_ref.shape[-1]
    y = _dot(a1_ref[...], w_ref[0:half, :]) + _dot(a2, w_ref[half:2 * half, :])
    mod = mod_ref[...]
    xn = x + mod[2:3] * y
    xo_ref[...] = xn
    h2 = _rms_rows(xn) * g_ref[...] * (1.0 + mod[4:5]) + mod[3:4]
    h_ref[...] = h2.astype(bf16)
    h_hi = h2.astype(bf16)
    h_lo = (h2 - h_hi.astype(f32)).astype(bf16)
    part = jnp.dot(h_hi, rw_ref[...], preferred_element_type=f32)
    logits = (part[:, 0:LANES] + part[:, LANES:2 * LANES]
              + jnp.dot(h_lo, rw_ref[:, 0:LANES], preferred_element_type=f32))
    scores = _sigmoid(logits.T[0:N_EXPERTS])
    e1, e2, w1, w2 = _route_rows(scores, scores + rb_ref[...])
    idx_ref[0:1, :] = e1
    idx_ref[1:2, :] = e2
    wt_ref[0:1, :] = w1
    wt_ref[1:2, :] = w2


def _out_proj(a1, a2s, x, modl, g, w, rw, rb, *, delta):
    if delta:
        b, l, d = x.shape
        x_specs, x_args, off = [pl.BlockSpec((None, TM, d), lambda bi, i: (bi, i + 1, 0))], [x], 1
    else:
        b, s, d = x[1].shape
        l = s + x[0].shape[1]
        x_specs, x_args, off = _ctx_lat_specs(d), list(x), 0
    nt = l // TM
    nto = nt - off
    half = a1.shape[-1]
    full = lambda shape: pl.BlockSpec(shape, lambda bi, i: (0,) * len(shape))
    row_l = lambda w_: pl.BlockSpec((None, TM, w_), lambda bi, i: (bi, i + off, 0))
    if delta:
        a2_specs = [row_l(half), row_l(half), row_l(half), full((1, LANES))]
        a2_args = list(a2s)
        a1_spec = pl.BlockSpec((None, TM, half), lambda bi, i: (bi, i, 0))
    else:
        a2_specs = [row_l(half)]
        a2_args = [a2s]
        a1_spec = row_l(half)
    t_out = b * nto * TM
    flat = lambda bi, i: (bi * nto + i, 0)
    return pl.pallas_call(
        functools.partial(_out_kernel, delta=delta),
        grid=(b, nto),
        in_specs=[a1_spec] + a2_specs + x_specs + [
            pl.BlockSpec((None, None, 6, d), lambda bi, i: (bi, jnp.minimum(i + off, 1), 0, 0)),
            full((1, d)), full((2 * half, d)), full((d, 2 * LANES)), full((N_EXPERTS, 1))],
        out_specs=[pl.BlockSpec((TM, d), flat), pl.BlockSpec((TM, d), flat),
                   pl.BlockSpec((2, TM), lambda bi, i: (0, bi * nto + i)),
                   pl.BlockSpec((2, TM), lambda bi, i: (0, bi * nto + i))],
        out_shape=[jax.ShapeDtypeStruct((t_out, d), f32), jax.ShapeDtypeStruct((t_out, d), bf16),
                   jax.ShapeDtypeStruct((2, t_out), i32), jax.ShapeDtypeStruct((2, t_out), f32)],
        compiler_params=_cparams(("arbitrary", "arbitrary")),
        name="out_proj_delta" if delta else "out_proj",
    )(a1, *a2_args, *x_args, modl, g, w, rw, rb)


MOE_GROUP = 8
MOE_BUF = 2 * TM + N_EXPERTS * MOE_GROUP
MOE_GROUPS = MOE_BUF // MOE_GROUP
FFN_TM = 2 * TM


def _dispatch_plan(idx):
    t = idx.shape[1]
    nt = t // TM
    e = idx.reshape(2, nt, TM).transpose(1, 0, 2).reshape(nt, 2 * TM)
    onehot = (e[:, :, None] == jnp.arange(N_EXPERTS, dtype=i32)).astype(i32)
    lrank = jnp.cumsum(onehot, axis=1) - onehot
    cnt = jnp.sum(onehot, axis=1)
    cnt_g = (cnt + MOE_GROUP - 1) // MOE_GROUP * MOE_GROUP
    boff = jnp.cumsum(cnt_g, axis=1) - cnt_g
    bufpos = jnp.sum((boff[:, None, :] + lrank) * onehot, axis=2)
    region = jnp.sum(cnt_g, axis=0)
    padded = (region + FFN_TM - 1) // FFN_TM * FFN_TM
    ends = jnp.cumsum(padded)
    base = ends - padded
    gstart = base[None, :] + jnp.cumsum(cnt_g, axis=0) - cnt_g
    n_tiles = -(-(2 * t + nt * N_EXPERTS * (MOE_GROUP - 1)) // FFN_TM) + N_EXPERTS
    n_used = (ends[-1] // FFN_TM).astype(i32)
    tiles = jnp.arange(n_tiles, dtype=i32)
    tile_e = jnp.sum((tiles[:, None] * FFN_TM >= ends[None, :]).astype(i32), axis=1)
    last_e = jnp.sum(((n_used - 1) * FFN_TM >= ends).astype(i32))
    tile_e = jnp.where(tiles < n_used, tile_e, last_e)
    g_row = jnp.arange(MOE_GROUPS, dtype=i32) * MOE_GROUP
    g_exp = jnp.sum((g_row[None, :, None] >= (boff + cnt_g)[:, None, :]).astype(i32), axis=2)
    g_sel = (g_exp[:, :, None] == jnp.arange(N_EXPERTS, dtype=i32)).astype(i32)
    g_dst = jnp.sum(g_sel * (gstart - boff)[:, None, :], axis=2) + g_row[None, :]
    runs = (g_dst.reshape(-1), jnp.sum(cnt_g, axis=1) // MOE_GROUP)
    gaps = (jnp.concatenate([base + region, ends[-1:]]),
            jnp.concatenate([(padded - region) // MOE_GROUP, n_tiles - n_used.reshape(1)]))
    return runs, gaps, bufpos.reshape(nt, 2, TM), tile_e, n_used.reshape(1), n_tiles


def _run_copies(runs, tile, buf, hbm, sem, *, to_hbm, start):
    row_ref, ng_ref = runs

    def body(c, carry):
        v = buf.at[pl.ds(pl.multiple_of(c * MOE_GROUP, MOE_GROUP) if start else 0, MOE_GROUP)]
        row = pl.multiple_of(row_ref[tile * MOE_GROUPS + c], MOE_GROUP) if start else 0
        h = hbm.at[pl.ds(row, MOE_GROUP)]
        cp = pltpu.make_async_copy(v, h, sem) if to_hbm else pltpu.make_async_copy(h, v, sem)
        if start:
            cp.start()
        else:
            cp.wait()
        return carry
    lax.fori_loop(0, ng_ref[tile], body, 0)


def _moe_dispatch_kernel(row_ref, ng_ref, gap0_ref, gapn_ref, h_ref, bp_ref, xs_hbm, buf, zbuf, sem, zsem):
    runs = (row_ref, ng_ref)
    i = pl.program_id(0)
    n = pl.num_programs(0)
    slot = i % 2
    bp = bp_ref[...]
    j = lax.broadcasted_iota(i32, (MOE_BUF, TM), 0)
    sel = jnp.where((j == bp[0:1]) | (j == bp[1:2]), 1.0, 0.0).astype(bf16)
    buf[slot] = jnp.dot(sel, h_ref[...], preferred_element_type=f32)
    _run_copies(runs, i, buf.at[slot], xs_hbm, sem.at[slot], to_hbm=True, start=True)

    @pl.when(i > 0)
    def _():
        _run_copies(runs, i - 1, buf.at[1 - slot], xs_hbm, sem.at[1 - slot], to_hbm=True, start=False)

    @pl.when(i == n - 1)
    def _():
        _run_copies(runs, i, buf.at[slot], xs_hbm, sem.at[slot], to_hbm=True, start=False)
        zbuf[...] = jnp.zeros(zbuf.shape, f32)
        for start in (True, False):
            for e in range(N_EXPERTS + 1):
                rows = MOE_GROUP if e < N_EXPERTS else FFN_TM

                def body(c, carry, e=e, start=start, rows=rows):
                    dst = xs_hbm.at[pl.ds(pl.multiple_of(gap0_ref[e] + c * rows, MOE_GROUP), rows)]
                    cp = pltpu.make_async_copy(zbuf.at[pl.ds(0, rows)], dst, zsem)
                    if start:
                        cp.start()
                    else:
                        cp.wait()
                    return carry
                lax.fori_loop(0, gapn_ref[e], body, 0)


def _moe_dispatch(h, runs, gaps, bufpos, n_tiles):
    t, d = h.shape
    grid_spec = pltpu.PrefetchScalarGridSpec(
        num_scalar_prefetch=4,
        grid=(t // TM,),
        in_specs=[pl.BlockSpec((TM, d), lambda i, *_: (i, 0)),
                  pl.BlockSpec((None, 2, TM), lambda i, *_: (i, 0, 0))],
        out_specs=pl.BlockSpec(memory_space=pl.ANY),
        scratch_shapes=[pltpu.VMEM((2, MOE_BUF, d), f32), pltpu.VMEM((FFN_TM, d), f32),
                        pltpu.SemaphoreType.DMA((2,)), pltpu.SemaphoreType.DMA],
    )
    return pl.pallas_call(
        _moe_dispatch_kernel,
        grid_spec=grid_spec,
        out_shape=jax.ShapeDtypeStruct((n_tiles * FFN_TM, d), f32),
        compiler_params=_cparams(("arbitrary",)),
        name="moe_dispatch",
    )(*runs, *gaps, h, bufpos)


def _moe_ffn_kernel(te_ref, nu_ref, x_ref, wg_ref, wu_ref, wd_ref, y_ref):
    @pl.when(pl.program_id(0) < nu_ref[0])
    def _():
        x = x_ref[...].astype(bf16)
        y = None
        for c in range(0, wg_ref.shape[1], TM):
            a = _silu(_dot(x, wg_ref[:, c:c + TM])) * _dot(x, wu_ref[:, c:c + TM])
            part = _dot(a, wd_ref[c:c + TM, :])
            y = part if y is None else y + part
        y_ref[...] = y

    @pl.when(pl.program_id(0) >= nu_ref[0])
    def _():
        y_ref[...] = jnp.zeros(y_ref.shape, f32)


def _moe_ffn(xs, tile_e, n_used, wg, wu, wd, layer):
    p, d = xs.shape
    ff = wg.shape[3]
    row = lambda i, te, nu: (jnp.minimum(i, nu[0] - 1), 0)
    grid_spec = pltpu.PrefetchScalarGridSpec(
        num_scalar_prefetch=2,
        grid=(p // FFN_TM,),
        in_specs=[pl.BlockSpec((FFN_TM, d), row),
                  pl.BlockSpec((None, None, d, ff), lambda i, te, nu: (layer, te[i], 0, 0)),
                  pl.BlockSpec((None, None, d, ff), lambda i, te, nu: (layer, te[i], 0, 0)),
                  pl.BlockSpec((None, None, ff, d), lambda i, te, nu: (layer, te[i], 0, 0))],
        out_specs=pl.BlockSpec((FFN_TM, d), lambda i, te, nu: (i, 0)),
    )
    return pl.pallas_call(
        _moe_ffn_kernel,
        grid_spec=grid_spec,
        out_shape=jax.ShapeDtypeStruct((p, d), f32),
        compiler_params=_cparams(("arbitrary",)),
        name="moe_ffn",
    )(tile_e, n_used, xs, wg, wu, wd)


def _moe_combine_kernel(row_ref, ng_ref, y_hbm, x_ref, bp_ref, wt_ref, mod_ref, *rest, final):
    if final:
        fg_ref, o_ref, ybuf, sem = rest
    else:
        o_ref, ybuf, sem = rest
    runs = (row_ref, ng_ref)
    i = pl.program_id(0)
    n = pl.num_programs(0)
    slot = i % 2

    @pl.when(i == 0)
    def _():
        ybuf[...] = jnp.zeros(ybuf.shape, f32)
        _run_copies(runs, 0, ybuf.at[0], y_hbm, sem.at[0], to_hbm=False, start=True)

    @pl.when(i + 1 < n)
    def _():
        _run_copies(runs, i + 1, ybuf.at[1 - slot], y_hbm, sem.at[1 - slot], to_hbm=False, start=True)

    _run_copies(runs, i, ybuf.at[slot], y_hbm, sem.at[slot], to_hbm=False, start=False)
    bp = bp_ref[...]
    w = wt_ref[...]
    j = lax.broadcasted_iota(i32, (MOE_BUF, TM), 0)
    sel_t = jnp.where(j == bp[0:1], w[0:1], 0.0) + jnp.where(j == bp[1:2], w[1:2], 0.0)
    out = x_ref[...] + mod_ref[5:6] * _dot_tn(sel_t, ybuf[slot])
    if final:
        out = _rms_rows(out) * fg_ref[...]
    o_ref[...] = out


def _moe_combine(y, runs, bufpos, x, wts, modl, tiles_per_batch, lat_only, final_g=None):
    t, d = x.shape
    n = t // TM
    off = 1 if lat_only else 0
    final = final_g is not None
    in_specs = [pl.BlockSpec(memory_space=pl.ANY),
                pl.BlockSpec((TM, d), lambda i, *_: (i, 0)),
                pl.BlockSpec((None, 2, TM), lambda i, *_: (i, 0, 0)),
                pl.BlockSpec((2, TM), lambda i, *_: (0, i)),
                pl.BlockSpec((None, None, 6, d),
                             lambda i, *_: (i // tiles_per_batch, jnp.minimum(i % tiles_per_batch + off, 1), 0, 0))]
    args = [y, x, bufpos, wts, modl]
    if final:
        in_specs.append(pl.BlockSpec((1, d), lambda i, *_: (0, 0)))
        args.append(final_g)
    grid_spec = pltpu.PrefetchScalarGridSpec(
        num_scalar_prefetch=2,
        grid=(n,),
        in_specs=in_specs,
        out_specs=pl.BlockSpec((TM, d), lambda i, *_: (i, 0)),
        scratch_shapes=[pltpu.VMEM((2, MOE_BUF, d), f32), pltpu.SemaphoreType.DMA((2,))],
    )
    return pl.pallas_call(
        functools.partial(_moe_combine_kernel, final=final),
        grid_spec=grid_spec,
        out_shape=jax.ShapeDtypeStruct((t, d), f32),
        compiler_params=_cparams(("arbitrary",)),
        name="moe_combine_final" if final else "moe_combine",
    )(*runs, *args)


def _moe(h, x, idx, wts, modl, experts, layer, tiles_per_batch, lat_only, final_g=None):
    runs, gaps, bufpos, tile_e, n_used, n_tiles = _dispatch_plan(idx)
    xs = _moe_dispatch(h, runs, gaps, bufpos, n_tiles)
    y = _moe_ffn(xs, tile_e, n_used, *experts, layer)
    return _moe_combine(y, runs, bufpos, x, wts, modl, tiles_per_batch, lat_only, final_g)


def _cd_in_kernel(x_ref, mod_ref, g_ref, w_ref, ag_ref, qkv_ref, dg_ref, ab_ref):
    mod = mod_ref[...]
    h = _rms_rows(x_ref[...]) * g_ref[...] * (1.0 + mod[1:2]) + mod[0:1]
    acc = _dot(h, w_ref[...])
    c1 = 2 * CV_CH
    c2 = c1 + DN_QKV
    c3 = c2 + DN_QK
    ag_ref[...] = acc[:, 0:c1]
    qkv_ref[...] = acc[:, c1:c2]
    dg_ref[...] = acc[:, c2:c3]
    ab_ref[...] = acc[:, c3:c3 + LANES]


def _cd_in_proj(x, modl, g, w):
    b, l, d = x.shape
    nt = l // TM
    n = w.shape[1]
    full = lambda shape: pl.BlockSpec(shape, lambda bi, i: (0,) * len(shape))
    widths = (2 * CV_CH, DN_QKV, DN_QK, LANES)
    return pl.pallas_call(
        _cd_in_kernel,
        grid=(b, nt),
        in_specs=[pl.BlockSpec((None, TM, d), lambda bi, i: (bi, i, 0)),
                  pl.BlockSpec((None, None, 6, d), _mod_index),
                  full((1, d)), full((d, n))],
        out_specs=[pl.BlockSpec((None, TM, w_), lambda bi, i: (bi, i, 0)) for w_ in widths],
        out_shape=[jax.ShapeDtypeStruct((b, l, w_), f32) for w_ in widths],
        compiler_params=_cparams(("arbitrary", "arbitrary")),
        name="cd_in_proj",
    )(x, modl, g, w)


CV_HALO = 16
CV_ROWS = 32


def _conformer_kernel(prev_ref, cur_ref, next_ref, w_ref, b_ref, lng_ref, lnb_ref, o_ref, ext_ref, sh_ref):
    i = pl.program_id(1)
    last = pl.num_programs(1) - 1

    def glu(z):
        return z[:, 0:CV_CH] * _sigmoid(z[:, CV_CH:2 * CV_CH])

    ext_ref[0:CV_HALO, :] = jnp.where(i > 0, glu(prev_ref[...]), 0.0)
    ext_ref[CV_HALO:CV_HALO + TM, :] = glu(cur_ref[...])
    ext_ref[CV_HALO + TM:2 * CV_HALO + TM, :] = jnp.where(i < last, glu(next_ref[...]), 0.0)
    n_sh = sh_ref.shape[1]
    for r in range(1, SUBLANES):
        sh_ref[r] = ext_ref[pl.ds(r, n_sh), :]
    base = CV_HALO - CV_WIDTH // 2
    groups = CV_ROWS // SUBLANES
    for rb in range(TM // CV_ROWS):
        acc = jnp.zeros((groups, SUBLANES, CV_CH), f32)
        for j in range(CV_WIDTH):
            a, r = divmod(base + j, SUBLANES)
            start = rb * CV_ROWS + a * SUBLANES
            x = ext_ref[pl.ds(start, CV_ROWS), :] if r == 0 else sh_ref[r, pl.ds(start, CV_ROWS), :]
            acc = acc + x.reshape(groups, SUBLANES, CV_CH) * w_ref[j][None]
        y = _layer_norm_rows(acc.reshape(CV_ROWS, CV_CH) + b_ref[...], lng_ref[...], lnb_ref[...])
        o_ref[rb * CV_ROWS:(rb + 1) * CV_ROWS, :] = _silu(y).astype(bf16)


def _conformer(ag, w, bias, lng, lnb):
    b, l, c2 = ag.shape
    nt = l // TM - 1
    hb = TM // CV_HALO
    n_halo = l // CV_HALO
    full = lambda shape: pl.BlockSpec(shape, lambda bi, i: (0,) * len(shape))
    return pl.pallas_call(
        _conformer_kernel,
        grid=(b, nt),
        in_specs=[pl.BlockSpec((None, CV_HALO, c2), lambda bi, i: (bi, (i + 1) * hb - 1, 0)),
                  pl.BlockSpec((None, TM, c2), lambda bi, i: (bi, i + 1, 0)),
                  pl.BlockSpec((None, CV_HALO, c2), lambda bi, i: (bi, jnp.minimum((i + 2) * hb, n_halo - 1), 0)),
                  full((CV_WIDTH, SUBLANES, CV_CH)), full((1, CV_CH)), full((1, CV_CH)), full((1, CV_CH))],
        out_specs=pl.BlockSpec((None, TM, CV_CH), lambda bi, i: (bi, i, 0)),
        out_shape=jax.ShapeDtypeStruct((b, nt * TM, CV_CH), bf16),
        scratch_shapes=[pltpu.VMEM((TM + 2 * CV_HALO, CV_CH), f32),
                        pltpu.VMEM((SUBLANES, TM + 2 * CV_HALO - SUBLANES, CV_CH), f32)],
        compiler_params=_cparams(("arbitrary", "arbitrary")),
        name="conformer",
    )(ag, ag, ag, w, bias, lng, lnb)


DN_HALO = 8


def _delta_feat_kernel(prev_ref, cur_ref, next_ref, ab_ref, w_ref, alog_ref, dtb_ref,
                       q_ref, k_ref, v_ref, g_ref, ext_ref):
    i = pl.program_id(1)
    last = pl.num_programs(1) - 1
    ext_ref[0:DN_HALO, :] = jnp.where(i > 1, prev_ref[...], 0.0)
    ext_ref[DN_HALO:DN_HALO + TM, :] = cur_ref[...]
    ext_ref[DN_HALO + TM:2 * DN_HALO + TM, :] = jnp.where((i > 0) & (i < last), next_ref[...], 0.0)
    w = w_ref[...]
    base = DN_HALO - DN_CONV // 2
    outs = (q_ref, k_ref, v_ref)
    for c in range(DN_QKV // LANES):
        cs = slice(c * LANES, (c + 1) * LANES)
        acc = jnp.zeros((TM, LANES), f32)
        for j in range(DN_CONV):
            acc = acc + ext_ref[pl.ds(base + j, TM), cs] * w[j:j + 1, cs]
        y = _silu(acc)
        part, h = divmod(c, DN_HEADS)
        if part < 2:
            y = y * lax.rsqrt(jnp.sum(y * y, axis=-1, keepdims=True) + EPS)
            if part == 0:
                y = y * (DN_DK ** -0.5)
        outs[part][:, h * LANES:(h + 1) * LANES] = y

    ab = ab_ref[...]
    g = -jnp.exp(alog_ref[...]) * (jnp.maximum(ab + dtb_ref[...], 0.0)
                                   + jnp.log(1.0 + jnp.exp(-jnp.abs(ab + dtb_ref[...]))))
    beta = _sigmoid(ab)
    lane = lax.broadcasted_iota(i32, ab.shape, 1)
    row = lax.broadcasted_iota(i32, (TM, TM), 0)
    col = lax.broadcasted_iota(i32, (TM, TM), 1)
    same = (row // DN_CHUNK) == (col // DN_CHUNK)
    for d in range(N_DIR):
        gd = g if d == 0 else pltpu.roll(g, LANES - DN_HEADS, 1)
        bd = pltpu.roll(beta, LANES - DN_HEADS * (d + 1), 1)
        tri = (same & ((col <= row) if d == 0 else (col >= row))).astype(f32)
        gc = jnp.dot(tri, jnp.where(lane < DN_HEADS, gd, 0.0), precision=lax.Precision.HIGHEST,
                     preferred_element_type=f32)
        g_ref[d] = jnp.where(lane < DN_HEADS, gc, bd)


def _delta_features(qkv, ab, w, alog, dtb):
    b, l, c = qkv.shape
    nt = l // TM
    hb = TM // DN_HALO
    n_halo = l // DN_HALO
    full = lambda shape: pl.BlockSpec(shape, lambda bi, i: (0,) * len(shape))
    return pl.pallas_call(
        _delta_feat_kernel,
        grid=(b, nt),
        in_specs=[pl.BlockSpec((None, DN_HALO, c), lambda bi, i: (bi, jnp.maximum(i * hb - 1, 0), 0)),
                  pl.BlockSpec((None, TM, c), lambda bi, i: (bi, i, 0)),
                  pl.BlockSpec((None, DN_HALO, c), lambda bi, i: (bi, jnp.minimum((i + 1) * hb, n_halo - 1), 0)),
                  pl.BlockSpec((None, TM, LANES), lambda bi, i: (bi, i, 0)),
                  full((DN_CONV, c)), full((1, LANES)), full((1, LANES))],
        out_specs=[pl.BlockSpec((None, TM, DN_QK), lambda bi, i: (bi, i, 0))] * 3
        + [pl.BlockSpec((N_DIR, None, TM, LANES), lambda bi, i: (0, bi, i, 0))],
        out_shape=[jax.ShapeDtypeStruct((b, l, DN_QK), f32)] * 3
        + [jax.ShapeDtypeStruct((N_DIR, b, l, LANES), f32)],
        scratch_shapes=[pltpu.VMEM((TM + 2 * DN_HALO, c), f32)],
        compiler_params=_cparams(("arbitrary", "arbitrary")),
        name="delta_features",
    )(qkv, qkv, qkv, ab, w, alog, dtb)


DN_PRE_CHUNKS = 4
DN_EG_ROWS = 8


def _delta_pre_kernel(q_ref, k_ref, v_ref, g_ref, u_ref, w_ref, qe_ref, kd_ref, qk_ref, eg_ref):
    n = DN_CHUNK
    row = lax.broadcasted_iota(i32, (n, LANES), 0)
    col = lax.broadcasted_iota(i32, (n, LANES), 1)
    eye = row == col
    valid = col < n
    zpad = jnp.zeros((LANES - n, LANES), f32)
    sh = [(s, h) for s in range(DN_PRE_CHUNKS) for h in range(DN_HEADS)]
    rsl = lambda s: slice(s * n, (s + 1) * n)
    csl = lambda h: slice(h * LANES, (h + 1) * LANES)
    q = [q_ref[rsl(s), csl(h)] for s, h in sh]
    k = [k_ref[rsl(s), csl(h)] for s, h in sh]
    v = [v_ref[rsl(s), csl(h)] for s, h in sh]
    k_pad = [jnp.concatenate([ki, zpad], axis=0) for ki in k]
    kk = [_dot_nt(ki, kp) for ki, kp in zip(k, k_pad)]
    qk = [_dot_nt(qi, kp) for qi, kp in zip(q, k_pad)]
    prob = [(i, d) for i in range(len(sh)) for d in range(N_DIR)]
    gc, beta, gl, decay, a = [], [], [], [], []
    for i, d in prob:
        s, h = sh[i]
        gates = g_ref[d, rsl(s), :]
        gci = gates[:, h:h + 1]
        bi = gates[:, DN_HEADS + h:DN_HEADS + h + 1]
        last = n - 1 if d == 0 else 0
        incl = ((col <= row) if d == 0 else (col >= row)) & valid
        strict = ((col < row) if d == 0 else (col > row)) & valid
        gc_row = jnp.sum(jnp.where(eye, gci, 0.0), axis=0, keepdims=True)
        di = jnp.exp(jnp.where(incl, gci - gc_row, -jnp.inf))
        gc.append(gci)
        beta.append(bi)
        gl.append(gates[last:last + 1, h:h + 1])
        decay.append(di)
        a.append(jnp.where(strict, -(bi * kk[i] * di), 0.0))
    y = list(a)
    p = list(a)
    for _ in range(5):
        p = [_dot(pi[:, 0:n], pi) for pi in p]
        yp = [_dot(yi[:, 0:n], pi) for yi, pi in zip(y, p)]
        y = [yi + pi + ypi for yi, pi, ypi in zip(y, p, yp)]
    e_gc = [jnp.exp(g) for g in gc]
    rhs = [jnp.concatenate([v[i] * beta[j], k[i] * (beta[j] * e_gc[j])], axis=1) for j, (i, d) in enumerate(prob)]
    sol = [r + _dot(yi[:, 0:n], r) for yi, r in zip(y, rhs)]
    for j, (i, d) in enumerate(prob):
        s, h = sh[i]
        rs, cs = rsl(s), csl(h)
        u_ref[d, rs, cs] = sol[j][:, 0:LANES]
        w_ref[d, rs, cs] = sol[j][:, LANES:2 * LANES].astype(bf16)
        qe_ref[d, rs, cs] = (q[i] * e_gc[j]).astype(bf16)
        kd_ref[d, rs, cs] = (k[i] * jnp.exp(gl[j] - gc[j])).astype(bf16)
        qk_ref[d, rs, cs] = (qk[i] * decay[j]).astype(bf16)
    for s in range(DN_PRE_CHUNKS):
        for d in range(N_DIR):
            last = s * n + (n - 1 if d == 0 else 0)
            eg_ref[d, s * DN_EG_ROWS:(s + 1) * DN_EG_ROWS, :] = jnp.broadcast_to(
                jnp.exp(g_ref[d, last:last + 1, :]), (DN_EG_ROWS, LANES))


def _delta_pre(q, k, v, gates):
    b, l, c = q.shape
    rows = DN_PRE_CHUNKS * DN_CHUNK
    n = l // rows
    row = pl.BlockSpec((None, rows, c), lambda bi, i: (bi, i, 0))
    out = pl.BlockSpec((N_DIR, None, rows, c), lambda bi, i: (0, bi, i, 0))
    eg_rows = DN_PRE_CHUNKS * DN_EG_ROWS
    return pl.pallas_call(
        _delta_pre_kernel,
        grid=(b, n),
        in_specs=[row, row, row, pl.BlockSpec((N_DIR, None, rows, LANES), lambda bi, i: (0, bi, i, 0))],
        out_specs=[out] * 5 + [pl.BlockSpec((N_DIR, None, eg_rows, LANES), lambda bi, i: (0, bi, i, 0))],
        out_shape=[jax.ShapeDtypeStruct((N_DIR, b, l, c), f32)]
        + [jax.ShapeDtypeStruct((N_DIR, b, l, c), bf16)] * 4
        + [jax.ShapeDtypeStruct((N_DIR, b, n * eg_rows, LANES), f32)],
        compiler_params=_cparams(("arbitrary", "arbitrary")),
        name="delta_pre",
    )(q, k, v, gates)


def _delta_scan_kernel(*refs):
    n_in = 6 * N_DIR
    o_refs = refs[n_in:n_in + N_DIR]
    s_ref = refs[n_in + N_DIR]
    n = DN_CHUNK
    n_sub = TM // DN_CHUNK

    @pl.when(pl.program_id(1) == 0)
    def _():
        s_ref[...] = jnp.zeros(s_ref.shape, f32)

    dh = [(d, h) for d in range(N_DIR) for h in range(DN_HEADS)]
    csl = lambda h: slice(h * LANES, (h + 1) * LANES)
    ins = lambda d: refs[6 * d:6 * d + 6]
    s = [s_ref[d, h] for d, h in dh]
    for c in range(n_sub):
        sub = lambda d: c if d == 0 else n_sub - 1 - c
        rsl = lambda d: slice(sub(d) * n, (sub(d) + 1) * n)
        r = [_dot(jnp.concatenate([ins(d)[1][rsl(d), csl(h)], ins(d)[2][rsl(d), csl(h)]], axis=0), si)
             for (d, h), si in zip(dh, s)]
        v_new = [ins(d)[0][rsl(d), csl(h)] - ri[0:n] for (d, h), ri in zip(dh, r)]
        s_add = [_dot_tn(ins(d)[3][rsl(d), csl(h)], vi) for (d, h), vi in zip(dh, v_new)]
        o_add = [_dot(ins(d)[4][rsl(d), h * LANES:h * LANES + n], vi) for (d, h), vi in zip(dh, v_new)]
        for j, (d, h) in enumerate(dh):
            o_refs[d][rsl(d), csl(h)] = r[j][n:2 * n] + o_add[j]
        s = [s[j] * ins(d)[5][sub(d) * DN_EG_ROWS:sub(d) * DN_EG_ROWS + 1, h:h + 1] + s_add[j]
             for j, (d, h) in enumerate(dh)]
    for j, (d, h) in enumerate(dh):
        s_ref[d, h] = s[j]


def _delta_scan(u, w, qe, kd, qk, eg):
    _, b, l, c = u.shape
    nt = l // TM
    eg_rows = TM // DN_CHUNK * DN_EG_ROWS

    def tile(d, ti):
        if d == 0:
            return ti
        return jnp.where(ti == 0, 0, nt - ti)

    in_specs, args = [], []
    for d in range(N_DIR):
        spec = pl.BlockSpec((None, None, TM, c), lambda bi, ti, d=d: (d, bi, tile(d, ti), 0))
        in_specs += [spec] * 5 + [pl.BlockSpec((None, None, eg_rows, LANES),
                                               lambda bi, ti, d=d: (d, bi, tile(d, ti), 0))]
        args += [u, w, qe, kd, qk, eg]
    return pl.pallas_call(
        _delta_scan_kernel,
        grid=(b, nt),
        in_specs=in_specs,
        out_specs=[pl.BlockSpec((None, TM, c), lambda bi, ti, d=d: (bi, tile(d, ti), 0))
                   for d in range(N_DIR)],
        out_shape=[jax.ShapeDtypeStruct((b, l, c), f32)] * N_DIR,
        scratch_shapes=[pltpu.VMEM((N_DIR, DN_HEADS, DN_DK, LANES), f32)],
        compiler_params=_cparams(("arbitrary", "arbitrary")),
        name="delta_scan",
    )(*args)


def _rope_tables(s, ctx_len):
    rows = s // GRID_W
    row = jnp.repeat(jnp.arange(rows, dtype=i32), GRID_W)
    col = jnp.tile(jnp.arange(GRID_W, dtype=i32), rows)
    freqs = ROPE_THETA ** (-jnp.arange(ROPE_PAIRS, dtype=f32) / ROPE_PAIRS)
    ang = jnp.stack([row, col], axis=-1).astype(f32)[..., None] * freqs
    cos = jnp.cos(ang)
    sin = jnp.sin(ang)
    cos_h = jnp.concatenate([cos[:, 0], cos[:, 0], cos[:, 1], cos[:, 1]], axis=-1)
    sin_h = jnp.concatenate([-sin[:, 0], sin[:, 0], -sin[:, 1], sin[:, 1]], axis=-1)
    cos_t = jnp.concatenate([jnp.ones((ctx_len, HEAD_DIM), f32), cos_h], axis=0)
    sin_t = jnp.concatenate([jnp.zeros((ctx_len, HEAD_DIM), f32), sin_h], axis=0)
    return jnp.tile(cos_t, (1, 2)), jnp.tile(sin_t, (1, 2))


def _pad_lanes(v, n=LANES):
    v = v.reshape(1, -1)
    return jnp.pad(v, ((0, 0), (0, n - v.shape[1])))


def kernel(x, c, ctx, c_ctx, mod_w, mod_b, norm1_g, norm2_g, ab_w_in, ab_q_norm, ab_k_norm, gm_ln_g, gm_ln_b, gm_w_s, gm_b_s, ab_w_out, cd_w_in, cv_dw_w, cv_dw_b, cv_ln_g, cv_ln_b, dn_conv_w, dn_a_log, dn_dt_bias, dn_o_norm, cd_w_out, router_w, router_b, moe_w_gate, moe_w_up, moe_w_down, final_norm_g):
    b, s, d = x.shape
    ctx_len = ctx.shape[1]
    assert ctx_len == TM and s % (4 * TM) == 0 and mod_w.shape[0] == 2
    l = ctx_len + s
    nt = l // TM

    n_rows = -(-(b + 1) // 8) * 8
    cc = jnp.concatenate([c, c_ctx[None, :], jnp.zeros((n_rows - b - 1, d), f32)], axis=0)
    mod = _modulation(cc, mod_w, mod_b)
    mod_lat = mod[:, :b].reshape(2, b, 1, 6, d)
    mod_ctx = jnp.broadcast_to(mod[:, b].reshape(2, 1, 1, 6, d), (2, b, 1, 6, d))
    modl = jnp.concatenate([mod_ctx, mod_lat], axis=2)

    rw32 =jnp.pad(router_w, ((0, 0), (0, LANES - N_EXPERTS)))
    rw_hi = rw32.astype(bf16)
    rw = jnp.concatenate([rw_hi, (rw32 - rw_hi.astype(f32)).astype(bf16)], axis=1)
    rb = router_b.reshape(N_EXPERTS, 1)
    experts = (moe_w_gate, moe_w_up, moe_w_down)

    cos_t, sin_t = _rope_tables(s, ctx_len)
    lane = jnp.arange(2 * LANES)
    seg = ((lane[:, None] // HEAD_DIM) == (lane[None, :] // HEAD_DIM)).astype(bf16) * (1.0 / HEAD_DIM)
    q, kd, vt, sg = _ab_in_proj(
        ctx, x, modl[0], norm1_g[0:1], ab_w_in[0].astype(bf16), seg,
        jnp.tile(ab_q_norm[0], 2)[None, :], jnp.tile(ab_k_norm[0], 2)[None, :], cos_t, sin_t,
        gm_ln_g[0:1], gm_ln_b[0:1], gm_w_s[0].astype(bf16),
        jnp.broadcast_to(gm_b_s[0][:, :, None], (GM_GROUPS, GM_CHUNK, LANES)))
    att = _attention(q, kd, vt)
    x1, h1, idx, wts = _out_proj(att, sg, (ctx, x), modl[0], norm2_g[0:1], ab_w_out[0].astype(bf16), rw, rb,
                                 delta=False)
    xs = _moe(h1, x1, idx, wts, modl[0], experts, 0, nt, False).reshape(b, l, d)

    c1 = 2 * CV_CH
    c2 = c1 + DN_QKV
    c3 = c2 + 2 * N_DIR * DN_HEADS
    w_in = cd_w_in[0]
    w_cd = jnp.concatenate([w_in[:, 0:c2], w_in[:, c3:], w_in[:, c2:c3],
                            jnp.zeros((d, LANES - (c3 - c2)), f32)], axis=1).astype(bf16)
    ag, qkv, dg, ab = _cd_in_proj(xs, modl[1], norm1_g[1:2], w_cd)
    conv_w = jnp.broadcast_to(cv_dw_w[0][:, None, :], (CV_WIDTH, SUBLANES, CV_CH))
    conv = _conformer(ag, conv_w, cv_dw_b[0:1], cv_ln_g[0:1], cv_ln_b[0:1])
    qn, kn, vv, gates = _delta_features(qkv, ab, dn_conv_w[0], _pad_lanes(dn_a_log[0]), _pad_lanes(dn_dt_bias[0]))
    o_f, o_b = _delta_scan(*_delta_pre(qn, kn, vv, gates))
    x2, h2, idx, wts = _out_proj(conv, (o_f, o_b, dg, dn_o_norm[0:1]), xs, modl[1], norm2_g[1:2],
                                 cd_w_out[0].astype(bf16), rw, rb, delta=True)
    out = _moe(h2, x2, idx, wts, modl[1], experts, 1, nt - 1, True, final_norm_g[None, :])
    return out.reshape(b, s, d)
```
